```python
import math
import jax, jax.numpy as jnp
from jax import lax
import numpy as np

D_MODEL = 1024
BATCH = 8
SEQ = 2048
DEPTH = 4
DEC_BATCH = 128
DEC_SEQ = 1
PAST_LEN = 16384
PAGE_SIZE = 128

N_CONV_LAYERS = (DEPTH + 1) // 2
N_SSM_LAYERS = DEPTH // 2
CONV_WIDTH = 31
EXPAND = 2
D_INNER = EXPAND * D_MODEL
HEAD_DIM_SSM = 64
N_HEADS_SSM = D_INNER // HEAD_DIM_SSM
N_GROUPS_SSM = 8
HEADS_PER_GROUP = N_HEADS_SSM // N_GROUPS_SSM
D_STATE = 128
SSM_CONV_WIDTH = 4
SSM_CONV_DIM = D_INNER + 2 * N_GROUPS_SSM * D_STATE
SSM_IN_DIM = D_INNER + SSM_CONV_DIM + N_HEADS_SSM
CHUNK = 128
D_FF = ((8 * D_MODEL // 3 + 255) // 256) * 256
PLE_DIM = 256
EPS = 1e-6

kernel_name = "hybrid_conformer_mamba2_decode_step"


def rmsnorm(x, g):
    xf = x.astype(jnp.float32)
    y = xf * lax.rsqrt(jnp.mean(xf * xf, axis=-1, keepdims=True) + EPS)
    return (y * g.astype(jnp.float32)).astype(x.dtype)


def layernorm(x, g, b):
    xf = x.astype(jnp.float32)
    mu = jnp.mean(xf, axis=-1, keepdims=True)
    var = jnp.mean(jnp.square(xf - mu), axis=-1, keepdims=True)
    y = (xf - mu) * lax.rsqrt(var + EPS) * g.astype(jnp.float32) + b.astype(jnp.float32)
    return y.astype(x.dtype)


def causal_dwconv(x, prefix, w, b):
    xp = jnp.concatenate([prefix.astype(x.dtype), x], axis=1)
    y = lax.conv_general_dilated(
        xp, w[:, None, :].astype(x.dtype), window_strides=(1,), padding='VALID',
        dimension_numbers=('NWC', 'WIO', 'NWC'), feature_group_count=x.shape[-1])
    return y + b, xp[:, -(w.shape[0] - 1):]


def conformer_mixer(u, buf, w_pw1, b_pw1, w_dw, b_dw, ln_g, ln_b, w_pw2, b_pw2):
    a = u @ w_pw1 + b_pw1
    v = jax.nn.glu(a, axis=-1)
    c, new_buf = causal_dwconv(v, buf, w_dw, b_dw)
    c = jax.nn.silu(layernorm(c, ln_g, ln_b))
    return c @ w_pw2 + b_pw2, new_buf


def ssd(x, dt, A, B, C, s0):
    b, l, g, r, p = x.shape
    n = B.shape[-1]
    q = CHUNK if l % CHUNK == 0 else l
    nc = l // q
    xdt = (x.astype(jnp.float32) * dt[..., None]).reshape(b, nc, q, g, r, p)
    Bc = B.astype(jnp.float32).reshape(b, nc, q, g, n)
    Cc = C.astype(jnp.float32).reshape(b, nc, q, g, n)
    a = jnp.moveaxis((dt * A).reshape(b, nc, q, g, r), 2, -1)
    a_cum = jnp.cumsum(a, axis=-1)
    seg = a_cum[..., :, None] - a_cum[..., None, :]
    causal = jnp.tril(jnp.ones((q, q), dtype=bool))
    Lmat = jnp.exp(jnp.where(causal, seg, -jnp.inf))
    CB = jnp.einsum('bcign,bcjgn->bcgij', Cc, Bc)
    y_diag = jnp.einsum('bcgrij,bcjgrp->bcigrp', CB[:, :, :, None] * Lmat, xdt)
    decay = jnp.exp(a_cum[..., -1:] - a_cum)
    states = jnp.einsum('bcjgn,bcgrj,bcjgrp->bcgrpn', Bc, decay, xdt)
    chunk_decay = jnp.exp(a_cum[..., -1])

    def step(s, inp):
        st, dec = inp
        return dec[..., None, None] * s + st, s

    final, prev = lax.scan(step, s0.astype(jnp.float32),
                           (jnp.moveaxis(states, 1, 0), jnp.moveaxis(chunk_decay, 1, 0)))
    prev = jnp.moveaxis(prev, 0, 1)
    y_off = jnp.einsum('bcign,bcgrpn,bcgri->bcigrp', Cc, prev, jnp.exp(a_cum))
    return (y_diag + y_off).reshape(b, l, g, r, p), final


def mamba2_mixer(u, ssm_state, conv_buf, w_in, w_conv, b_conv, dt_bias, a_log, d_skip, norm_g, w_out):
    b, l, _ = u.shape
    zxbcdt = u @ w_in
    z = zxbcdt[..., :D_INNER]
    xbc = zxbcdt[..., D_INNER:D_INNER + SSM_CONV_DIM]
    dt = zxbcdt[..., D_INNER + SSM_CONV_DIM:]
    xbc, new_conv = causal_dwconv(xbc, conv_buf, w_conv, b_conv)
    xbc = jax.nn.silu(xbc)
    gn = N_GROUPS_SSM * D_STATE
    xs = xbc[..., :D_INNER].reshape(b, l, N_GROUPS_SSM, HEADS_PER_GROUP, HEAD_DIM_SSM)
    Bm = xbc[..., D_INNER:D_INNER + gn].reshape(b, l, N_GROUPS_SSM, D_STATE)
    Cm = xbc[..., D_INNER + gn:].reshape(b, l, N_GROUPS_SSM, D_STATE)
    dt = jax.nn.softplus(dt.astype(jnp.float32) + dt_bias.astype(jnp.float32))
    dt = dt.reshape(b, l, N_GROUPS_SSM, HEADS_PER_GROUP)
    A = -jnp.exp(a_log.astype(jnp.float32)).reshape(N_GROUPS_SSM, HEADS_PER_GROUP)
    s0 = ssm_state.reshape(b, N_GROUPS_SSM, HEADS_PER_GROUP, HEAD_DIM_SSM, D_STATE)
    y, final = ssd(xs, dt, A, Bm, Cm, s0)
    y = y + d_skip.astype(jnp.float32).reshape(N_GROUPS_SSM, HEADS_PER_GROUP)[:, :, None] * xs.astype(jnp.float32)
    yg = y.reshape(b, l, D_INNER) * jax.nn.silu(z.astype(jnp.float32))
    yg = yg.reshape(b, l, N_GROUPS_SSM, D_INNER // N_GROUPS_SSM)
    yg = yg * lax.rsqrt(jnp.mean(yg * yg, axis=-1, keepdims=True) + EPS)
    yg = yg.reshape(b, l, D_INNER) * norm_g.astype(jnp.float32)
    out = yg.astype(u.dtype) @ w_out
    new_state = final.reshape(b, N_HEADS_SSM, HEAD_DIM_SSM, D_STATE)
    return out, new_state, new_conv


def swiglu(u, w_gate, w_up, w_down):
    return (jax.nn.silu(u @ w_gate) * (u @ w_up)) @ w_down


def trunk(x, p, conv_bufs, ssm_states, ssm_bufs,
          g_mix, g_ffn, g_ple, g_final,
          cv_w_pw1, cv_b_pw1, cv_w_dw, cv_b_dw, cv_ln_g, cv_ln_b, cv_w_pw2, cv_b_pw2,
          ssm_w_in, ssm_w_conv, ssm_b_conv, ssm_dt_bias, ssm_a_log, ssm_d, ssm_norm_g, ssm_w_out,
          ffn_w_gate, ffn_w_up, ffn_w_down, ple_w_proj, ple_w_gate):
    h = x
    new_conv, new_ssm, new_ssm_conv = [], [], []
    for i in range(DEPTH):
        u = rmsnorm(h, g_mix[i])
        j = i // 2
        if i % 2 == 0:
            o, nb = conformer_mixer(u, conv_bufs[j], cv_w_pw1[j], cv_b_pw1[j], cv_w_dw[j], cv_b_dw[j],
                                    cv_ln_g[j], cv_ln_b[j], cv_w_pw2[j], cv_b_pw2[j])
            new_conv.append(nb)
        else:
            o, ns, nb = mamba2_mixer(u, ssm_states[j], ssm_bufs[j], ssm_w_in[j], ssm_w_conv[j], ssm_b_conv[j],
                                     ssm_dt_bias[j], ssm_a_log[j], ssm_d[j], ssm_norm_g[j], ssm_w_out[j])
            new_ssm.append(ns)
            new_ssm_conv.append(nb)
        h = h + o
        h = h + swiglu(rmsnorm(h, g_ffn[i]), ffn_w_gate[i], ffn_w_up[i], ffn_w_down[i])
        gate = jax.nn.sigmoid(rmsnorm(h, g_ple[i]) @ ple_w_gate[i])
        h = h + (p[i] @ ple_w_proj[i]) * gate
    return rmsnorm(h, g_final), jnp.stack(new_conv), jnp.stack(new_ssm), jnp.stack(new_ssm_conv)


def setup_inputs(seed: int = 0) -> dict:
    key = jax.random.key(seed)
    ks = iter(jax.random.split(key, 64))
    f32 = jnp.float32

    def nrm(shape, scale):
        return jax.random.normal(next(ks), shape, f32) * scale

    def gain(shape):
        return 1.0 + 0.05 * jax.random.normal(next(ks), shape, f32)

    NC, NS = N_CONV_LAYERS, N_SSM_LAYERS
    out_scale = 1.0 / math.sqrt(2 * DEPTH)
    inp = {}
    inp['x_prompt'] = nrm((BATCH, SEQ, D_MODEL), 1.0)
    inp['x_sample'] = nrm((DEC_BATCH, DEC_SEQ, D_MODEL), 1.0)
    inp['p_prompt'] = nrm((DEPTH, BATCH, SEQ, PLE_DIM), 1.0)
    inp['p_sample'] = nrm((DEPTH, DEC_BATCH, DEC_SEQ, PLE_DIM), 1.0)
    inp['state_conv_mixer'] = nrm((NC, DEC_BATCH, CONV_WIDTH - 1, D_MODEL), 0.5)
    inp['state_ssm'] = nrm((NS, DEC_BATCH, N_HEADS_SSM, HEAD_DIM_SSM, D_STATE), 0.1)
    inp['state_ssm_conv'] = nrm((NS, DEC_BATCH, SSM_CONV_WIDTH - 1, SSM_CONV_DIM), 0.5)
    inp['g_mix'] = gain((DEPTH, D_MODEL))
    inp['g_ffn'] = gain((DEPTH, D_MODEL))
    inp['g_ple'] = gain((DEPTH, D_MODEL))
    inp['g_final'] = gain((D_MODEL,))
    inp['cv_w_pw1'] = nrm((NC, D_MODEL, 2 * D_MODEL), D_MODEL ** -0.5)
    inp['cv_b_pw1'] = nrm((NC, 2 * D_MODEL), 0.02)
    inp['cv_w_dw'] = nrm((NC, CONV_WIDTH, D_MODEL), CONV_WIDTH ** -0.5)
    inp['cv_b_dw'] = nrm((NC, D_MODEL), 0.02)
    inp['cv_ln_g'] = gain((NC, D_MODEL))
    inp['cv_ln_b'] = nrm((NC, D_MODEL), 0.02)
    inp['cv_w_pw2'] = nrm((NC, D_MODEL, D_MODEL), D_MODEL ** -0.5 * out_scale)
    inp['cv_b_pw2'] = nrm((NC, D_MODEL), 0.02)
    inp['ssm_w_in'] = nrm((NS, D_MODEL, SSM_IN_DIM), D_MODEL ** -0.5)
    inp['ssm_w_conv'] = nrm((NS, SSM_CONV_WIDTH, SSM_CONV_DIM), SSM_CONV_WIDTH ** -0.5)
    inp['ssm_b_conv'] = nrm((NS, SSM_CONV_DIM), 0.02)
    dt0 = jnp.exp(jax.random.uniform(next(ks), (NS, N_HEADS_SSM), f32, math.log(1e-3), math.log(1e-1)))
    inp['ssm_dt_bias'] = dt0 + jnp.log(-jnp.expm1(-dt0))
    inp['ssm_a_log'] = jnp.log(jax.random.uniform(next(ks), (NS, N_HEADS_SSM), f32, 1.0, 16.0))
    inp['ssm_d'] = gain((NS, N_HEADS_SSM))
    inp['ssm_norm_g'] = gain((NS, D_INNER))
    inp['ssm_w_out'] = nrm((NS, D_INNER, D_MODEL), D_INNER ** -0.5 * out_scale)
    inp['ffn_w_gate'] = nrm((DEPTH, D_MODEL, D_FF), D_MODEL ** -0.5)
    inp['ffn_w_up'] = nrm((DEPTH, D_MODEL, D_FF), D_MODEL ** -0.5)
    inp['ffn_w_down'] = nrm((DEPTH, D_FF, D_MODEL), D_FF ** -0.5 * out_scale)
    inp['ple_w_proj'] = nrm((DEPTH, PLE_DIM, D_MODEL), PLE_DIM ** -0.5 * out_scale)
    inp['ple_w_gate'] = nrm((DEPTH, D_MODEL, D_MODEL), D_MODEL ** -0.5)
    return inp


def reference(x_prompt, x_sample, p_prompt, p_sample, state_conv_mixer, state_ssm, state_ssm_conv,
              g_mix, g_ffn, g_ple, g_final,
              cv_w_pw1, cv_b_pw1, cv_w_dw, cv_b_dw, cv_ln_g, cv_ln_b, cv_w_pw2, cv_b_pw2,
              ssm_w_in, ssm_w_conv, ssm_b_conv, ssm_dt_bias, ssm_a_log, ssm_d, ssm_norm_g, ssm_w_out,
              ffn_w_gate, ffn_w_up, ffn_w_down, ple_w_proj, ple_w_gate):
    weights = (g_mix, g_ffn, g_ple, g_final,
               cv_w_pw1, cv_b_pw1, cv_w_dw, cv_b_dw, cv_ln_g, cv_ln_b, cv_w_pw2, cv_b_pw2,
               ssm_w_in, ssm_w_conv, ssm_b_conv, ssm_dt_bias, ssm_a_log, ssm_d, ssm_norm_g, ssm_w_out,
               ffn_w_gate, ffn_w_up, ffn_w_down, ple_w_proj, ple_w_gate)
    bp = x_prompt.shape[0]
    conv0 = jnp.zeros((N_CONV_LAYERS, bp, CONV_WIDTH - 1, D_MODEL), x_prompt.dtype)
    ssm0 = jnp.zeros((N_SSM_LAYERS, bp, N_HEADS_SSM, HEAD_DIM_SSM, D_STATE), jnp.float32)
    ssmc0 = jnp.zeros((N_SSM_LAYERS, bp, SSM_CONV_WIDTH - 1, SSM_CONV_DIM), x_prompt.dtype)
    y_prompt, nc_p, ns_p, nsc_p = trunk(x_prompt, p_prompt, conv0, ssm0, ssmc0, *weights)
    y_sample, nc_s, ns_s, nsc_s = trunk(x_sample, p_sample, state_conv_mixer, state_ssm, state_ssm_conv, *weights)
    return (y_prompt, y_sample, nc_p, ns_p, nsc_p, nc_s, ns_s, nsc_s)
```

```python
import functools

import jax
import jax.numpy as jnp
from jax import lax
from jax.experimental import pallas as pl
from jax.experimental.pallas import tpu as pltpu

F32 = jnp.float32
BF16 = jnp.bfloat16

D_MODEL = 1024
CONV_WIDTH = 31
D_INNER = 2048
HEAD_DIM = 64
N_HEADS = 32
N_GROUPS = 8
HEADS_PER_GROUP = 4
D_STATE = 128
SSM_CONV_WIDTH = 4
SSM_CONV_DIM = 4096
D_FF = 2816
PLE_DIM = 256
EPS = 1e-6

LANES = 128
SUBLANES = 8
GROUP_W = HEADS_PER_GROUP * HEAD_DIM
FF_CHUNK = 256
SSD_CHUNK = 128
CONV_HALO = 32
SSM_HALO = 8
CONV_ROWS = 16
VMEM_LIMIT = 60 * 1024 * 1024


def _cparams(n_axes):
    return pltpu.CompilerParams(
        dimension_semantics=("arbitrary",) * n_axes, vmem_limit_bytes=VMEM_LIMIT)


def _sigmoid(x):
    return 1.0 / (1.0 + jnp.exp(-x))


def _silu(x):
    return x * _sigmoid(x)


def _softplus(x):
    return jnp.maximum(x, 0.0) + jnp.log(1.0 + jnp.exp(-jnp.abs(x)))


def _rms(x, g):
    return x * lax.rsqrt(jnp.mean(x * x, axis=-1, keepdims=True) + EPS) * g


def _dot(a, b):
    return jnp.dot(a, b, preferred_element_type=F32)


def _dot_nt(a, b):
    return lax.dot_general(a, b, (((1,), (1,)), ((), ())), preferred_element_type=F32)


def _dot_tn(a, b):
    return lax.dot_general(a, b, (((0,), (0,)), ((), ())), preferred_element_type=F32)


def _split3(x):
    hi = x.astype(BF16)
    r1 = x - hi.astype(F32)
    mid = r1.astype(BF16)
    lo = (r1 - mid.astype(F32)).astype(BF16)
    return hi, mid, lo


def _full(shape):
    return pl.BlockSpec(shape, lambda *_: (0,) * len(shape))


def _pw1_glu_kernel(h_ref, g_ref, w_ref, b_ref, v_ref):
    u = _rms(h_ref[...], g_ref[...]).astype(BF16)
    a = _dot(u, w_ref[...]) + b_ref[...]
    v_ref[...] = a[:, :D_MODEL] * _sigmoid(a[:, D_MODEL:])


def _pw1_glu(h, g, w, b, tm):
    m = h.shape[0]
    return pl.pallas_call(
        _pw1_glu_kernel,
        grid=(m // tm,),
        in_specs=[pl.BlockSpec((tm, D_MODEL), lambda i: (i, 0)),
                  _full((1, D_MODEL)), _full((D_MODEL, 2 * D_MODEL)), _full((1, 2 * D_MODEL))],
        out_specs=pl.BlockSpec((tm, D_MODEL), lambda i: (i, 0)),
        out_shape=jax.ShapeDtypeStruct((m, D_MODEL), F32),
        compiler_params=_cparams(1),
        name="pw1_glu",
    )(h, g, w, b)


def _ln_silu_pw2(c, h, lng, lnb, w2, b2):
    mu = jnp.mean(c, axis=-1, keepdims=True)
    d = c - mu
    var = jnp.mean(d * d, axis=-1, keepdims=True)
    y = _silu(d * lax.rsqrt(var + EPS) * lng + lnb)
    return h + _dot(y.astype(BF16), w2) + b2


def _conv_prompt_kernel(v_ref, h_ref, wdw_ref, bdw_ref, lng_ref, lnb_ref, w2_ref, b2_ref,
                        o_ref, xbuf, cbuf, *, tl):
    l = pl.program_id(1)

    @pl.when(l == 0)
    def _():
        xbuf[0, 0:CONV_HALO, :] = jnp.zeros((CONV_HALO, D_MODEL), F32)

    @pl.when(l > 0)
    def _():
        xbuf[0, 0:CONV_HALO, :] = xbuf[0, tl:tl + CONV_HALO, :]

    xbuf[0, CONV_HALO:CONV_HALO + tl, :] = v_ref[...]
    ext = tl + CONV_HALO - SUBLANES
    for j in range(1, SUBLANES):
        xbuf[j, 0:ext, :] = xbuf[0, j:j + ext, :]
    off = CONV_HALO - (CONV_WIDTH - 1)

    def body(i, carry):
        r0 = pl.multiple_of(i * CONV_ROWS, CONV_ROWS)
        acc = jnp.broadcast_to(bdw_ref[...], (CONV_ROWS, D_MODEL))
        for k in range(CONV_WIDTH):
            j = (k + off) % SUBLANES
            base = (k + off) - j
            acc = acc + xbuf[j, pl.ds(r0 + base, CONV_ROWS), :] * wdw_ref[k:k + 1, :]
        cbuf[pl.ds(r0, CONV_ROWS), :] = acc
        return carry

    lax.fori_loop(0, tl // CONV_ROWS, body, 0)
    o_ref[...] = _ln_silu_pw2(cbuf[...], h_ref[...], lng_ref[...], lnb_ref[...],
                              w2_ref[...], b2_ref[...])


def _conv_prompt(v, h, wdw, bdw, lng, lnb, w2, b2, tl):
    bsz, seq, _ = v.shape
    tile = pl.BlockSpec((None, tl, D_MODEL), lambda b, l: (b, l, 0))
    return pl.pallas_call(
        functools.partial(_conv_prompt_kernel, tl=tl),
        grid=(bsz, seq // tl),
        in_specs=[tile, tile, _full((CONV_WIDTH, D_MODEL)), _full((1, D_MODEL)),
                  _full((1, D_MODEL)), _full((1, D_MODEL)), _full((D_MODEL, D_MODEL)),
                  _full((1, D_MODEL))],
        out_specs=tile,
        out_shape=jax.ShapeDtypeStruct(v.shape, F32),
        scratch_shapes=[pltpu.VMEM((SUBLANES, tl + CONV_HALO, D_MODEL), F32),
                        pltpu.VMEM((tl, D_MODEL), F32)],
        compiler_params=_cparams(2),
        name="conv_prompt",
    )(v, h, wdw, bdw, lng, lnb, w2, b2)


def _conv_step_kernel(v_ref, h_ref, st_ref, wdw_ref, bdw_ref, lng_ref, lnb_ref, w2_ref, b2_ref,
                      o_ref, nst_ref):
    v = v_ref[...]
    nk = CONV_WIDTH - 1
    acc = bdw_ref[...] + v * wdw_ref[nk:nk + 1, :]
    for k in range(nk):
        acc = acc + st_ref[:, k * D_MODEL:(k + 1) * D_MODEL] * wdw_ref[k:k + 1, :]
    nst_ref[:, 0:(nk - 1) * D_MODEL] = st_ref[:, D_MODEL:nk * D_MODEL]
    nst_ref[:, (nk - 1) * D_MODEL:nk * D_MODEL] = v
    o_ref[...] = _ln_silu_pw2(acc, h_ref[...], lng_ref[...], lnb_ref[...],
                              w2_ref[...], b2_ref[...])


def _conv_step(v, h, st, wdw, bdw, lng, lnb, w2, b2, tb):
    bsz = v.shape[0]
    sw = (CONV_WIDTH - 1) * D_MODEL
    row = pl.BlockSpec((tb, D_MODEL), lambda i: (i, 0))
    strow = pl.BlockSpec((tb, sw), lambda i: (i, 0))
    return pl.pallas_call(
        _conv_step_kernel,
        grid=(bsz // tb,),
        in_specs=[row, row, strow, _full((CONV_WIDTH, D_MODEL)), _full((1, D_MODEL)),
                  _full((1, D_MODEL)), _full((1, D_MODEL)), _full((D_MODEL, D_MODEL)),
                  _full((1, D_MODEL))],
        out_specs=[row, strow],
        out_shape=[jax.ShapeDtypeStruct((bsz, D_MODEL), F32),
                   jax.ShapeDtypeStruct((bsz, sw), F32)],
        compiler_params=_cparams(1),
        name="conv_step",
    )(v, h, st, wdw, bdw, lng, lnb, w2, b2)


def _ffn_ple_kernel(h_ref, p_ref, gf_ref, wg_ref, wu_ref, wd_ref, gp_ref, wpg_ref, wpp_ref,
                    gfin_ref, o_ref, *, final):
    h = h_ref[...]
    u = _rms(h, gf_ref[...]).astype(BF16)
    acc = h
    for f in range(0, D_FF, FF_CHUNK):
        gate = _dot(u, wg_ref[:, f:f + FF_CHUNK])
        up = _dot(u, wu_ref[:, f:f + FF_CHUNK])
        a = (_silu(gate) * up).astype(BF16)
        acc = acc + _dot(a, wd_ref[f:f + FF_CHUNK, :])
    pg = _sigmoid(_dot(_rms(acc, gp_ref[...]).astype(BF16), wpg_ref[...]))
    out = acc + _dot(p_ref[...].astype(BF16), wpp_ref[...]) * pg
    if final:
        out = _rms(out, gfin_ref[...])
    o_ref[...] = out


def _ffn_ple(h, p, gf, wg, wu, wd, gp, wpg, wpp, gfin, tm, final):
    m = h.shape[0]
    return pl.pallas_call(
        functools.partial(_ffn_ple_kernel, final=final),
        grid=(m // tm,),
        in_specs=[pl.BlockSpec((tm, D_MODEL), lambda i: (i, 0)),
                  pl.BlockSpec((tm, PLE_DIM), lambda i: (i, 0)),
                  _full((1, D_MODEL)), _full((D_MODEL, D_FF)), _full((D_MODEL, D_FF)),
                  _full((D_FF, D_MODEL)), _full((1, D_MODEL)), _full((D_MODEL, D_MODEL)),
                  _full((PLE_DIM, D_MODEL)), _full((1, D_MODEL))],
        out_specs=pl.BlockSpec((tm, D_MODEL), lambda i: (i, 0)),
        out_shape=jax.ShapeDtypeStruct((m, D_MODEL), F32),
        compiler_params=_cparams(1),
        name="ffn_ple",
    )(h, p, gf, wg, wu, wd, gp, wpg, wpp, gfin)


def _gated_norm_out(y, z, h, ng, wout):
    yg = y * _silu(z)
    parts = []
    for g in range(N_GROUPS):
        blk = yg[:, g * GROUP_W:(g + 1) * GROUP_W]
        ms = jnp.mean(blk * blk, axis=-1, keepdims=True)
        parts.append(blk * lax.rsqrt(ms + EPS))
    yn = jnp.concatenate(parts, axis=-1) * ng
    return h + _dot(yn.astype(BF16), wout)


def _lane_block_ids(shape, width):
    return lax.broadcasted_iota(jnp.int32, shape, 1) // width


def _expand_heads(v, g, rows):
    blk = _lane_block_ids((rows, GROUP_W), HEAD_DIM)
    out = jnp.zeros((rows, GROUP_W), F32)
    for r in range(HEADS_PER_GROUP):
        hh = g * HEADS_PER_GROUP + r
        col = jnp.broadcast_to(v[:, hh:hh + 1], (rows, GROUP_W))
        out = jnp.where(blk == r, col, out)
    return out


def _mamba_prompt_kernel(h_ref, g_ref, wzx_ref, wdt_ref, dtb_ref, alog_ref, wcv_ref, bcv_ref,
                         dsk_ref, ng_ref, wout_ref,
                         o_ref, st_ref, cs_ref,
                         z_scr, xb_scr, x_scr, b_scr, c_scr, dt_scr, y_scr, s_scr, *, tl):
    l = pl.program_id(1)
    q = SSD_CHUNK

    @pl.when(l == 0)
    def _():
        s_scr[...] = jnp.zeros(s_scr.shape, F32)
        xb_scr[0:SSM_HALO, :] = jnp.zeros((SSM_HALO, SSM_CONV_DIM), F32)

    @pl.when(l > 0)
    def _():
        xb_scr[0:SSM_HALO, :] = xb_scr[tl:tl + SSM_HALO, :]

    h = h_ref[...]
    u = _rms(h, g_ref[...]).astype(BF16)
    z_scr[...] = _dot(u, wzx_ref[:, :D_INNER])
    xb_scr[SSM_HALO:SSM_HALO + tl, :] = _dot(u, wzx_ref[:, D_INNER:])
    dt_scr[...] = _softplus(_dot(u, wdt_ref[...]) + dtb_ref[...])

    off = SSM_HALO - (SSM_CONV_WIDTH - 1)
    cs_ref[...] = xb_scr[tl + off:tl + SSM_HALO, :]
    conv = jnp.broadcast_to(bcv_ref[...], (tl, SSM_CONV_DIM))
    for k in range(SSM_CONV_WIDTH):
        conv = conv + xb_scr[k + off:k + off + tl, :] * wcv_ref[k:k + 1, :]
    act = _silu(conv)
    x_scr[...] = act[:, :D_INNER]
    b_scr[...] = act[:, D_INNER:D_INNER + N_GROUPS * D_STATE].astype(BF16)
    c_scr[...] = act[:, D_INNER + N_GROUPS * D_STATE:].astype(BF16)

    a_neg = -jnp.exp(alog_ref[...])
    ri = lax.broadcasted_iota(jnp.int32, (q, q), 0)
    ci = lax.broadcasted_iota(jnp.int32, (q, q), 1)
    causal = ri >= ci
    tril = causal.astype(BF16)

    def chunk(c, carry):
        r0 = pl.multiple_of(c * q, q)
        rows = pl.ds(r0, q)
        dtc = dt_scr[rows, :]
        a = dtc * a_neg
        a_hi, a_mid, a_lo = _split3(a)
        a_cum = _dot(tril, a_hi) + _dot(tril, a_mid) + _dot(tril, a_lo)
        a_cum_t = a_cum.T
        a_last = a_cum[q - 1:q, :]
        ea = jnp.exp(a_cum)
        dec = jnp.exp(a_last - a_cum)
        for g in range(N_GROUPS):
            gs = slice(g * GROUP_W, (g + 1) * GROUP_W)
            ns = slice(g * D_STATE, (g + 1) * D_STATE)
            bg = b_scr[rows, ns]
            cg = c_scr[rows, ns]
            xg = x_scr[rows, gs]
            sg = s_scr[gs, :]
            cb = _dot_nt(cg, bg)
            y_off = _dot_nt(cg, sg.astype(BF16))
            xdt = xg * _expand_heads(dtc, g, q)
            yd = []
            for r in range(HEADS_PER_GROUP):
                hh = g * HEADS_PER_GROUP + r
                seg = a_cum[:, hh:hh + 1] - a_cum_t[hh:hh + 1, :]
                lm = jnp.exp(jnp.where(causal, seg, -jnp.inf))
                m = (cb * lm).astype(BF16)
                yd.append(_dot(m, xdt[:, r * HEAD_DIM:(r + 1) * HEAD_DIM].astype(BF16)))
            y = (jnp.concatenate(yd, axis=-1) + y_off * _expand_heads(ea, g, q)
                 + xg * dsk_ref[:, gs])
            y_scr[rows, gs] = y
            xs = (xdt * _expand_heads(dec, g, q)).astype(BF16)
            add = _dot_tn(xs, bg)
            for r in range(HEADS_PER_GROUP):
                hh = g * HEADS_PER_GROUP + r
                cd = jnp.exp(jnp.broadcast_to(a_cum_t[hh:hh + 1, q - 1:q], (HEAD_DIM, D_STATE)))
                hs = slice(hh * HEAD_DIM, (hh + 1) * HEAD_DIM)
                s_scr[hs, :] = cd * s_scr[hs, :] + add[r * HEAD_DIM:(r + 1) * HEAD_DIM, :]
        return carry

    lax.fori_loop(0, tl // q, chunk, 0)
    st_ref[...] = s_scr[...]
    o_ref[...] = _gated_norm_out(y_scr[...], z_scr[...], h, ng_ref[...], wout_ref[...])


def _mamba_prompt(h, g, wzx, wdt, dtb, alog, wcv, bcv, dsk, ng, wout, tl):
    bsz, seq, _ = h.shape
    tile = pl.BlockSpec((None, tl, D_MODEL), lambda b, l: (b, l, 0))
    nconv = SSM_CONV_WIDTH - 1
    return pl.pallas_call(
        functools.partial(_mamba_prompt_kernel, tl=tl),
        grid=(bsz, seq // tl),
        in_specs=[tile, _full((1, D_MODEL)), _full((D_MODEL, D_INNER + SSM_CONV_DIM)),
                  _full((D_MODEL, LANES)), _full((1, LANES)), _full((1, LANES)),
                  _full((SSM_CONV_WIDTH, SSM_CONV_DIM)), _full((1, SSM_CONV_DIM)),
                  _full((1, D_INNER)), _full((1, D_INNER)), _full((D_INNER, D_MODEL))],
        out_specs=[tile,
                   pl.BlockSpec((None, D_INNER, D_STATE), lambda b, l: (b, 0, 0)),
                   pl.BlockSpec((None, nconv, SSM_CONV_DIM), lambda b, l: (b, 0, 0))],
        out_shape=[jax.ShapeDtypeStruct(h.shape, F32),
                   jax.ShapeDtypeStruct((bsz, D_INNER, D_STATE), F32),
                   jax.ShapeDtypeStruct((bsz, nconv, SSM_CONV_DIM), F32)],
        scratch_shapes=[pltpu.VMEM((tl, D_INNER), F32),
                        pltpu.VMEM((tl + SSM_HALO, SSM_CONV_DIM), F32),
                        pltpu.VMEM((tl, D_INNER), F32),
                        pltpu.VMEM((tl, N_GROUPS * D_STATE), BF16),
                        pltpu.VMEM((tl, N_GROUPS * D_STATE), BF16),
                        pltpu.VMEM((tl, LANES), F32),
                        pltpu.VMEM((tl, D_INNER), F32),
                        pltpu.VMEM((D_INNER, D_STATE), F32)],
        compiler_params=_cparams(2),
        name="mamba_prompt",
    )(h, g, wzx, wdt, dtb, alog, wcv, bcv, dsk, ng, wout)


def _mamba_step_in_kernel(h_ref, g_ref, wzx_ref, wdt_ref, dtb_ref, alog_ref, wcv_ref, bcv_ref,
                          cs_ref, z_ref, act_ref, dt_ref, da_ref, ncs_ref):
    u = _rms(h_ref[...], g_ref[...]).astype(BF16)
    z_ref[...] = _dot(u, wzx_ref[:, :D_INNER])
    xb = _dot(u, wzx_ref[:, D_INNER:])
    dt = _softplus(_dot(u, wdt_ref[...]) + dtb_ref[...])
    dt_ref[...] = dt
    da_ref[...] = jnp.exp(dt * (-jnp.exp(alog_ref[...])))
    nk = SSM_CONV_WIDTH - 1
    cw = SSM_CONV_DIM
    conv = bcv_ref[...] + xb * wcv_ref[nk:nk + 1, :]
    for k in range(nk):
        conv = conv + cs_ref[:, k * cw:(k + 1) * cw] * wcv_ref[k:k + 1, :]
    act_ref[...] = _silu(conv)
    ncs_ref[:, 0:(nk - 1) * cw] = cs_ref[:, cw:nk * cw]
    ncs_ref[:, (nk - 1) * cw:nk * cw] = xb


def _mamba_step_in(h, g, wzx, wdt, dtb, alog, wcv, bcv, cs):
    bsz = h.shape[0]
    cw = (SSM_CONV_WIDTH - 1) * SSM_CONV_DIM
    shapes = [(bsz, D_INNER), (bsz, SSM_CONV_DIM), (bsz, LANES), (bsz, LANES), (bsz, cw)]
    return pl.pallas_call(
        _mamba_step_in_kernel,
        grid=(1,),
        in_specs=[_full((bsz, D_MODEL)), _full((1, D_MODEL)),
                  _full((D_MODEL, D_INNER + SSM_CONV_DIM)), _full((D_MODEL, LANES)),
                  _full((1, LANES)), _full((1, LANES)),
                  _full((SSM_CONV_WIDTH, SSM_CONV_DIM)), _full((1, SSM_CONV_DIM)),
                  _full((bsz, cw))],
        out_specs=[_full(s) for s in shapes],
        out_shape=[jax.ShapeDtypeStruct(s, F32) for s in shapes],
        compiler_params=_cparams(1),
        name="mamba_step_in",
    )(h, g, wzx, wdt, dtb, alog, wcv, bcv, cs)


def _ssm_step_kernel(da_ref, s_ref, act_ref, dt_ref, dsk_ref, ns_ref, y_ref, *, tb):
    i = pl.program_id(0)
    rid = lax.broadcasted_iota(jnp.int32, (tb, 1), 0)
    zpad = jnp.zeros((tb, GROUP_W), F32)
    dt = dt_ref[...]
    for g in range(N_GROUPS):
        gs = slice(g * GROUP_W, (g + 1) * GROUP_W)
        xg = act_ref[:, gs]
        bg = act_ref[:, D_INNER + g * D_STATE:D_INNER + (g + 1) * D_STATE]
        cg = act_ref[:, D_INNER + (N_GROUPS + g) * D_STATE:D_INNER + (N_GROUPS + g + 1) * D_STATE]
        xdt = xg * _expand_heads(dt, g, tb)
        y8 = jnp.zeros((tb, GROUP_W), F32)
        for r in range(tb):
            sel = rid == r
            x1 = jnp.concatenate([jnp.where(sel, xdt, 0.0), zpad], axis=0).astype(BF16)
            b1 = jnp.concatenate([jnp.where(sel, bg, 0.0), zpad[:, :D_STATE]], axis=0).astype(BF16)
            c1 = jnp.concatenate([jnp.where(sel, cg, 0.0), zpad[:, :D_STATE]], axis=0).astype(BF16)
            add = _dot_tn(x1, b1)
            news = []
            for k in range(HEADS_PER_GROUP):
                hh = g * HEADS_PER_GROUP + k
                da = da_ref[i * tb + r, hh]
                sn = da * s_ref[r, hh] + add[k * HEAD_DIM:(k + 1) * HEAD_DIM, :]
                ns_ref[r, hh] = sn
                news.append(sn)
            sg = jnp.concatenate(news, axis=0).astype(BF16)
            y8 = y8 + _dot_nt(c1, sg)[:tb, :]
        y_ref[:, gs] = y8 + xg * dsk_ref[:, gs]


def _ssm_step(da, s, act, dt, dsk, tb):
    bsz = s.shape[0]
    sblk = pl.BlockSpec((tb, N_HEADS, HEAD_DIM, D_STATE), lambda i: (i, 0, 0, 0))
    return pl.pallas_call(
        functools.partial(_ssm_step_kernel, tb=tb),
        grid=(bsz // tb,),
        in_specs=[pl.BlockSpec(memory_space=pltpu.SMEM), sblk,
                  pl.BlockSpec((tb, SSM_CONV_DIM), lambda i: (i, 0)),
                  pl.BlockSpec((tb, LANES), lambda i: (i, 0)),
                  _full((1, D_INNER))],
        out_specs=[sblk, pl.BlockSpec((tb, D_INNER), lambda i: (i, 0))],
        out_shape=[jax.ShapeDtypeStruct(s.shape, F32),
                   jax.ShapeDtypeStruct((bsz, D_INNER), F32)],
        compiler_params=_cparams(1),
        name="ssm_step",
    )(da, s, act, dt, dsk)


def _gnorm_out_kernel(y_ref, z_ref, h_ref, ng_ref, wout_ref, o_ref):
    o_ref[...] = _gated_norm_out(y_ref[...], z_ref[...], h_ref[...], ng_ref[...], wout_ref[...])


def _gnorm_out(y, z, h, ng, wout):
    bsz = y.shape[0]
    return pl.pallas_call(
        _gnorm_out_kernel,
        grid=(1,),
        in_specs=[_full((bsz, D_INNER)), _full((bsz, D_INNER)), _full((bsz, D_MODEL)),
                  _full((1, D_INNER)), _full((D_INNER, D_MODEL))],
        out_specs=_full((bsz, D_MODEL)),
        out_shape=jax.ShapeDtypeStruct((bsz, D_MODEL), F32),
        compiler_params=_cparams(1),
        name="gnorm_out",
    )(y, z, h, ng, wout)


def _row(x):
    return x.reshape(1, -1)


def _pad_lanes(x):
    return jnp.pad(x, [(0, 0)] * (x.ndim - 1) + [(0, LANES - x.shape[-1])])


def kernel(x_prompt, x_sample, p_prompt, p_sample, state_conv_mixer, state_ssm, state_ssm_conv, g_mix, g_ffn, g_ple, g_final, cv_w_pw1, cv_b_pw1, cv_w_dw, cv_b_dw, cv_ln_g, cv_ln_b, cv_w_pw2, cv_b_pw2, ssm_w_in, ssm_w_conv, ssm_b_conv, ssm_dt_bias, ssm_a_log, ssm_d, ssm_norm_g, ssm_w_out, ffn_w_gate, ffn_w_up, ffn_w_down, ple_w_proj, ple_w_gate):
    depth = g_mix.shape[0]
    bp, seq, _ = x_prompt.shape
    bs = x_sample.shape[0]
    mp = bp * seq
    tm_prompt = 512
    tl_conv = 512
    tl_ssm = 256
    zx_w = D_INNER + SSM_CONV_DIM

    hp = x_prompt
    hs = x_sample.reshape(bs, D_MODEL)
    conv_p, conv_s, ssm_p, ssm_s, ssmc_p, ssmc_s = [], [], [], [], [], []
    gfin = _row(g_final)

    for i in range(depth):
        j = i // 2
        gm = _row(g_mix[i])
        if i % 2 == 0:
            w1 = cv_w_pw1[j].astype(BF16)
            b1 = _row(cv_b_pw1[j])
            w2 = cv_w_pw2[j].astype(BF16)
            cargs = (cv_w_dw[j], _row(cv_b_dw[j]), _row(cv_ln_g[j]), _row(cv_ln_b[j]),
                     w2, _row(cv_b_pw2[j]))
            vp = _pw1_glu(hp.reshape(mp, D_MODEL), gm, w1, b1, tm_prompt).reshape(bp, seq, D_MODEL)
            conv_p.append(vp[:, seq - (CONV_WIDTH - 1):, :])
            hp = _conv_prompt(vp, hp, *cargs, tl_conv)
            vs = _pw1_glu(hs, gm, w1, b1, bs)
            st = state_conv_mixer[j].reshape(bs, (CONV_WIDTH - 1) * D_MODEL)
            hs, nst = _conv_step(vs, hs, st, *cargs, 32)
            conv_s.append(nst.reshape(bs, CONV_WIDTH - 1, D_MODEL))
        else:
            w_in = ssm_w_in[j]
            wzx = w_in[:, :zx_w].astype(BF16)
            wdt = _pad_lanes(w_in[:, zx_w:]).astype(BF16)
            dtb = _pad_lanes(_row(ssm_dt_bias[j]))
            alog = _pad_lanes(_row(ssm_a_log[j]))
            wcv = ssm_w_conv[j]
            bcv = _row(ssm_b_conv[j])
            dsk = _row(jnp.repeat(ssm_d[j], HEAD_DIM))
            ng = _row(ssm_norm_g[j])
            wout = ssm_w_out[j].astype(BF16)
            hp, fin, ncs = _mamba_prompt(hp, gm, wzx, wdt, dtb, alog, wcv, bcv, dsk, ng, wout, tl_ssm)
            ssm_p.append(fin.reshape(bp, N_HEADS, HEAD_DIM, D_STATE))
            ssmc_p.append(ncs)
            cs = state_ssm_conv[j].reshape(bs, (SSM_CONV_WIDTH - 1) * SSM_CONV_DIM)
            z, act, dt, da, ncs_s = _mamba_step_in(hs, gm, wzx, wdt, dtb, alog, wcv, bcv, cs)
            ns, y = _ssm_step(da[:, :N_HEADS], state_ssm[j], act, dt, dsk, 8)
            hs = _gnorm_out(y, z, hs, ng, wout)
            ssm_s.append(ns)
            ssmc_s.append(ncs_s.reshape(bs, SSM_CONV_WIDTH - 1, SSM_CONV_DIM))
        fargs = (_row(g_ffn[i]), ffn_w_gate[i].astype(BF16), ffn_w_up[i].astype(BF16),
                 ffn_w_down[i].astype(BF16), _row(g_ple[i]), ple_w_gate[i].astype(BF16),
                 ple_w_proj[i].astype(BF16), gfin)
        final = i == depth - 1
        hp = _ffn_ple(hp.reshape(mp, D_MODEL), p_prompt[i].reshape(mp, PLE_DIM), *fargs,
                      tm_prompt, final).reshape(bp, seq, D_MODEL)
        hs = _ffn_ple(hs, p_sample[i].reshape(bs, PLE_DIM), *fargs, bs, final)

    return (hp, hs.reshape(bs, 1, D_MODEL),
            jnp.stack(conv_p), jnp.stack(ssm_p), jnp.stack(ssmc_p),
            jnp.stack(conv_s), jnp.stack(ssm_s), jnp.stack(ssmc_s))
```

```python
import functools

import jax
import jax.numpy as jnp
from jax import lax
from jax.experimental import pallas as pl
from jax.experimental.pallas import tpu as pltpu

F32 = jnp.float32
BF16 = jnp.bfloat16

D_MODEL = 1024
CONV_WIDTH = 31
D_INNER = 2048
HEAD_DIM = 64
N_HEADS = 32
N_GROUPS = 8
HEADS_PER_GROUP = 4
D_STATE = 128
SSM_CONV_WIDTH = 4
SSM_CONV_DIM = 4096
D_FF = 2816
PLE_DIM = 256
EPS = 1e-6

LANES = 128
SUBLANES = 8
GROUP_W = HEADS_PER_GROUP * HEAD_DIM
FF_CHUNK = 256
SSD_CHUNK = 128
CONV_HALO = 32
SSM_HALO = 8
CONV_ROWS = 16
VMEM_LIMIT = 60 * 1024 * 1024


def _cparams(n_axes):
    return pltpu.CompilerParams(
        dimension_semantics=("arbitrary",) * n_axes, vmem_limit_bytes=VMEM_LIMIT)


def _sigmoid(x):
    return 1.0 / (1.0 + jnp.exp(-x))


def _silu(x):
    return x * _sigmoid(x)


def _softplus(x):
    return jnp.maximum(x, 0.0) + jnp.log(1.0 + jnp.exp(-jnp.abs(x)))


def _rms(x, g):
    return x * lax.rsqrt(jnp.mean(x * x, axis=-1, keepdims=True) + EPS) * g


def _dot(a, b):
    return jnp.dot(a, b, preferred_element_type=F32)


def _dot_nt(a, b):
    return lax.dot_general(a, b, (((1,), (1,)), ((), ())), preferred_element_type=F32)


def _dot_tn(a, b):
    return lax.dot_general(a, b, (((0,), (0,)), ((), ())), preferred_element_type=F32)


def _split3(x):
    hi = x.astype(BF16)
    r1 = x - hi.astype(F32)
    mid = r1.astype(BF16)
    lo = (r1 - mid.astype(F32)).astype(BF16)
    return hi, mid, lo


def _full(shape):
    return pl.BlockSpec(shape, lambda *_: (0,) * len(shape))


def _pw1_glu_kernel(h_ref, g_ref, w_ref, b_ref, v_ref):
    u = _rms(h_ref[...], g_ref[...]).astype(BF16)
    a = _dot(u, w_ref[...]) + b_ref[...]
    v_ref[...] = a[:, :D_MODEL] * _sigmoid(a[:, D_MODEL:])


def _pw1_glu(h, g, w, b, tm):
    m = h.shape[0]
    return pl.pallas_call(
        _pw1_glu_kernel,
        grid=(m // tm,),
        in_specs=[pl.BlockSpec((tm, D_MODEL), lambda i: (i, 0)),
                  _full((1, D_MODEL)), _full((D_MODEL, 2 * D_MODEL)), _full((1, 2 * D_MODEL))],
        out_specs=pl.BlockSpec((tm, D_MODEL), lambda i: (i, 0)),
        out_shape=jax.ShapeDtypeStruct((m, D_MODEL), F32),
        compiler_params=_cparams(1),
        name="pw1_glu",
    )(h, g, w, b)


def _ln_silu_pw2(c, h, lng, lnb, w2, b2):
    mu = jnp.mean(c, axis=-1, keepdims=True)
    d = c - mu
    var = jnp.mean(d * d, axis=-1, keepdims=True)
    y = _silu(d * lax.rsqrt(var + EPS) * lng + lnb)
    return h + _dot(y.astype(BF16), w2) + b2


def _conv_prompt_kernel(v_ref, h_ref, wdw_ref, bdw_ref, lng_ref, lnb_ref, w2_ref, b2_ref,
                        o_ref, xbuf, cbuf, *, tl):
    l = pl.program_id(1)

    @pl.when(l == 0)
    def _():
        xbuf[0, 0:CONV_HALO, :] = jnp.zeros((CONV_HALO, D_MODEL), F32)

    @pl.when(l > 0)
    def _():
        xbuf[0, 0:CONV_HALO, :] = xbuf[0, tl:tl + CONV_HALO, :]

    xbuf[0, CONV_HALO:CONV_HALO + tl, :] = v_ref[...]
    ext = tl + CONV_HALO - SUBLANES
    for j in range(1, SUBLANES):
        xbuf[j, 0:ext, :] = xbuf[0, j:j + ext, :]
    off = CONV_HALO - (CONV_WIDTH - 1)

    def body(i, carry):
        r0 = pl.multiple_of(i * CONV_ROWS, CONV_ROWS)
        acc = jnp.broadcast_to(bdw_ref[...], (CONV_ROWS, D_MODEL))
        for k in range(CONV_WIDTH):
            j = (k + off) % SUBLANES
            base = (k + off) - j
            acc = acc + xbuf[j, pl.ds(r0 + base, CONV_ROWS), :] * wdw_ref[k:k + 1, :]
        cbuf[pl.ds(r0, CONV_ROWS), :] = acc
        return carry

    lax.fori_loop(0, tl // CONV_ROWS, body, 0)
    o_ref[...] = _ln_silu_pw2(cbuf[...], h_ref[...], lng_ref[...], lnb_ref[...],
                              w2_ref[...], b2_ref[...])


def _conv_prompt(v, h, wdw, bdw, lng, lnb, w2, b2, tl):
    bsz, seq, _ = v.shape
    tile = pl.BlockSpec((None, tl, D_MODEL), lambda b, l: (b, l, 0))
    return pl.pallas_call(
        functools.partial(_conv_prompt_kernel, tl=tl),
        grid=(bsz, seq // tl),
        in_specs=[tile, tile, _full((CONV_WIDTH, D_MODEL)), _full((1, D_MODEL)),
                  _full((1, D_MODEL)), _full((1, D_MODEL)), _full((D_MODEL, D_MODEL)),
                  _full((1, D_MODEL))],
        out_specs=tile,
        out_shape=jax.ShapeDtypeStruct(v.shape, F32),
        scratch_shapes=[pltpu.VMEM((SUBLANES, tl + CONV_HALO, D_MODEL), F32),
                        pltpu.VMEM((tl, D_MODEL), F32)],
        compiler_params=_cparams(2),
        name="conv_prompt",
    )(v, h, wdw, bdw, lng, lnb, w2, b2)


def _conv_step_kernel(v_ref, h_ref, st_ref, wdw_ref, bdw_ref, lng_ref, lnb_ref, w2_ref, b2_ref,
                      o_ref, nst_ref):
    v = v_ref[...]
    nk = CONV_WIDTH - 1
    acc = bdw_ref[...] + v * wdw_ref[nk:nk + 1, :]
    for k in range(nk):
        acc = acc + st_ref[:, k * D_MODEL:(k + 1) * D_MODEL] * wdw_ref[k:k + 1, :]
    nst_ref[:, 0:(nk - 1) * D_MODEL] = st_ref[:, D_MODEL:nk * D_MODEL]
    nst_ref[:, (nk - 1) * D_MODEL:nk * D_MODEL] = v
    o_ref[...] = _ln_silu_pw2(acc, h_ref[...], lng_ref[...], lnb_ref[...],
                              w2_ref[...], b2_ref[...])


def _conv_step(v, h, st, wdw, bdw, lng, lnb, w2, b2, tb):
    bsz = v.shape[0]
    sw = (CONV_WIDTH - 1) * D_MODEL
    row = pl.BlockSpec((tb, D_MODEL), lambda i: (i, 0))
    strow = pl.BlockSpec((tb, sw), lambda i: (i, 0))
    return pl.pallas_call(
        _conv_step_kernel,
        grid=(bsz // tb,),
        in_specs=[row, row, strow, _full((CONV_WIDTH, D_MODEL)), _full((1, D_MODEL)),
                  _full((1, D_MODEL)), _full((1, D_MODEL)), _full((D_MODEL, D_MODEL)),
                  _full((1, D_MODEL))],
        out_specs=[row, strow],
        out_shape=[jax.ShapeDtypeStruct((bsz, D_MODEL), F32),
                   jax.ShapeDtypeStruct((bsz, sw), F32)],
        compiler_params=_cparams(1),
        name="conv_step",
    )(v, h, st, wdw, bdw, lng, lnb, w2, b2)


def _ffn_ple_kernel(h_ref, p_ref, gf_ref, wg_ref, wu_ref, wd_ref, gp_ref, wpg_ref, wpp_ref,
                    gfin_ref, o_ref, *, final):
    h = h_ref[...]
    u = _rms(h, gf_ref[...]).astype(BF16)
    acc = h
    for f in range(0, D_FF, FF_CHUNK):
        gate = _dot(u, wg_ref[:, f:f + FF_CHUNK])
        up = _dot(u, wu_ref[:, f:f + FF_CHUNK])
        a = (_silu(gate) * up).astype(BF16)
        acc = acc + _dot(a, wd_ref[f:f + FF_CHUNK, :])
    pg = _sigmoid(_dot(_rms(acc, gp_ref[...]).astype(BF16), wpg_ref[...]))
    out = acc + _dot(p_ref[...].astype(BF16), wpp_ref[...]) * pg
    if final:
        out = _rms(out, gfin_ref[...])
    o_ref[...] = out


def _ffn_ple(h, p, gf, wg, wu, wd, gp, wpg, wpp, gfin, tm, final):
    m = h.shape[0]
    return pl.pallas_call(
        functools.partial(_ffn_ple_kernel, final=final),
        grid=(m // tm,),
        in_specs=[pl.BlockSpec((tm, D_MODEL), lambda i: (i, 0)),
                  pl.BlockSpec((tm, PLE_DIM), lambda i: (i, 0)),
                  _full((1, D_MODEL)), _full((D_MODEL, D_FF)), _full((D_MODEL, D_FF)),
                  _full((D_FF, D_MODEL)), _full((1, D_MODEL)), _full((D_MODEL, D_MODEL)),
                  _full((PLE_DIM, D_MODEL)), _full((1, D_MODEL))],
        out_specs=pl.BlockSpec((tm, D_MODEL), lambda i: (i, 0)),
        out_shape=jax.ShapeDtypeStruct((m, D_MODEL), F32),
        compiler_params=_cparams(1),
        name="ffn_ple",
    )(h, p, gf, wg, wu, wd, gp, wpg, wpp, gfin)


def _gated_norm_out(y, z, h, ng, wout):
    yg = y * _silu(z)
    parts = []
    for g in range(N_GROUPS):
        blk = yg[:, g * GROUP_W:(g + 1) * GROUP_W]
        ms = jnp.mean(blk * blk, axis=-1, keepdims=True)
        parts.append(blk * lax.rsqrt(ms + EPS))
    yn = jnp.concatenate(parts, axis=-1) * ng
    return h + _dot(yn.astype(BF16), wout)


def _lane_block_ids(shape, width):
    return lax.broadcasted_iota(jnp.int32, shape, 1) // width


def _expand_heads(v, g, rows):
    blk = _lane_block_ids((rows, GROUP_W), HEAD_DIM)
    out = jnp.zeros((rows, GROUP_W), F32)
    for r in range(HEADS_PER_GROUP):
        hh = g * HEADS_PER_GROUP + r
        col = jnp.broadcast_to(v[:, hh:hh + 1], (rows, GROUP_W))
        out = jnp.where(blk == r, col, out)
    return out


def _mamba_prompt_kernel(h_ref, g_ref, wzx_ref, wdt_ref, dtb_ref, alog_ref, wcv_ref, bcv_ref,
                         dsk_ref, ng_ref, wout_ref,
                         o_ref, st_ref, cs_ref,
                         z_scr, xb_scr, x_scr, b_scr, c_scr, dt_scr, y_scr, s_scr, *, tl):
    l = pl.program_id(1)
    q = SSD_CHUNK

    @pl.when(l == 0)
    def _():
        s_scr[...] = jnp.zeros(s_scr.shape, F32)
        xb_scr[0:SSM_HALO, :] = jnp.zeros((SSM_HALO, SSM_CONV_DIM), F32)

    @pl.when(l > 0)
    def _():
        xb_scr[0:SSM_HALO, :] = xb_scr[tl:tl + SSM_HALO, :]

    h = h_ref[...]
    u = _rms(h, g_ref[...]).astype(BF16)
    z_scr[...] = _dot(u, wzx_ref[:, :D_INNER])
    xb_scr[SSM_HALO:SSM_HALO + tl, :] = _dot(u, wzx_ref[:, D_INNER:])
    dt_scr[...] = _softplus(_dot(u, wdt_ref[...]) + dtb_ref[...])

    off = SSM_HALO - (SSM_CONV_WIDTH - 1)
    cs_ref[...] = xb_scr[tl + off:tl + SSM_HALO, :]
    cw = 4 * LANES
    for cb0 in range(0, SSM_CONV_DIM, cw):
        cl = slice(cb0, cb0 + cw)
        xw = xb_scr[:, cl]
        nk = SSM_CONV_WIDTH - 1
        conv = bcv_ref[:, cl] + xw[SSM_HALO:, :] * wcv_ref[nk:nk + 1, cl]
        for s in range(1, SSM_CONV_WIDTH):
            conv = conv + pltpu.roll(xw, s, 0)[SSM_HALO:, :] * wcv_ref[nk - s:nk - s + 1, cl]
        act = _silu(conv)
        if cb0 < D_INNER:
            x_scr[:, cl] = act
        elif cb0 < D_INNER + N_GROUPS * D_STATE:
            b_scr[:, cb0 - D_INNER:cb0 - D_INNER + cw] = act.astype(BF16)
        else:
            c0 = cb0 - D_INNER - N_GROUPS * D_STATE
            c_scr[:, c0:c0 + cw] = act.astype(BF16)

    a_neg = -jnp.exp(alog_ref[...])
    ri = lax.broadcasted_iota(jnp.int32, (q, q), 0)
    ci = lax.broadcasted_iota(jnp.int32, (q, q), 1)
    causal = ri >= ci
    tril = causal.astype(BF16)
    low_half = lax.broadcasted_iota(jnp.int32, (q, LANES), 1) < HEAD_DIM
    blk = _lane_block_ids((q, GROUP_W), HEAD_DIM)

    def pair_expand(cols):
        return jnp.concatenate([jnp.where(low_half, cols[0], cols[1]),
                                jnp.where(low_half, cols[2], cols[3])], axis=-1)

    for c in range(tl // q):
        rows = slice(c * q, (c + 1) * q)
        dtc = dt_scr[rows, :]
        a = dtc * a_neg
        a_hi, a_mid, a_lo = _split3(a)
        a_cum = _dot(tril, a_hi) + _dot(tril, a_mid) + _dot(tril, a_lo)
        a_cum_t = a_cum.T
        for g in range(N_GROUPS):
            gs = slice(g * GROUP_W, (g + 1) * GROUP_W)
            ns = slice(g * D_STATE, (g + 1) * D_STATE)
            heads = [g * HEADS_PER_GROUP + r for r in range(HEADS_PER_GROUP)]
            bg = b_scr[rows, ns]
            cg = c_scr[rows, ns]
            xg = x_scr[rows, gs]
            sg = s_scr[gs, :]
            cb = _dot_nt(cg, bg)
            y_off = _dot_nt(cg, sg.astype(BF16))
            bc_a = [jnp.broadcast_to(a_cum[:, hh:hh + 1], (q, LANES)) for hh in heads]
            bc_dt = [jnp.broadcast_to(dtc[:, hh:hh + 1], (q, LANES)) for hh in heads]
            a_e = pair_expand(bc_a)
            xdt = xg * pair_expand(bc_dt)
            ms = []
            for r, hh in enumerate(heads):
                seg = bc_a[r] - a_cum_t[hh:hh + 1, :]
                lm = jnp.exp(jnp.where(causal, seg, -jnp.inf))
                ms.append((cb * lm).astype(BF16))
            xblk = jnp.concatenate(
                [jnp.where(blk == r, xdt, 0.0).astype(BF16) for r in range(HEADS_PER_GROUP)], axis=0)
            y = (_dot(jnp.concatenate(ms, axis=-1), xblk) + y_off * jnp.exp(a_e)
                 + xg * dsk_ref[:, gs])
            y_scr[rows, gs] = y
            xs = (xdt * jnp.exp(a_e[q - 1:q, :] - a_e)).astype(BF16)
            add = _dot_tn(xs, bg)
            for r, hh in enumerate(heads):
                cd = jnp.exp(bc_a[r][q - 1:q, :])
                hs = slice(hh * HEAD_DIM, (hh + 1) * HEAD_DIM)
                s_scr[hs, :] = cd * s_scr[hs, :] + add[r * HEAD_DIM:(r + 1) * HEAD_DIM, :]

    st_ref[...] = s_scr[...]
    o_ref[...] = _gated_norm_out(y_scr[...], z_scr[...], h, ng_ref[...], wout_ref[...])


def _mamba_prompt(h, g, wzx, wdt, dtb, alog, wcv, bcv, dsk, ng, wout, tl):
    bsz, seq, _ = h.shape
    tile = pl.BlockSpec((None, tl, D_MODEL), lambda b, l: (b, l, 0))
    nconv = SSM_CONV_WIDTH - 1
    return pl.pallas_call(
        functools.partial(_mamba_prompt_kernel, tl=tl),
        grid=(bsz, seq // tl),
        in_specs=[tile, _full((1, D_MODEL)), _full((D_MODEL, D_INNER + SSM_CONV_DIM)),
                  _full((D_MODEL, LANES)), _full((1, LANES)), _full((1, LANES)),
                  _full((SSM_CONV_WIDTH, SSM_CONV_DIM)), _full((1, SSM_CONV_DIM)),
                  _full((1, D_INNER)), _full((1, D_INNER)), _full((D_INNER, D_MODEL))],
        out_specs=[tile,
                   pl.BlockSpec((None, D_INNER, D_STATE), lambda b, l: (b, 0, 0)),
                   pl.BlockSpec((None, nconv, SSM_CONV_DIM), lambda b, l: (b, 0, 0))],
        out_shape=[jax.ShapeDtypeStruct(h.shape, F32),
                   jax.ShapeDtypeStruct((bsz, D_INNER, D_STATE), F32),
                   jax.ShapeDtypeStruct((bsz, nconv, SSM_CONV_DIM), F32)],
        scratch_shapes=[pltpu.VMEM((tl, D_INNER), F32),
                        pltpu.VMEM((tl + SSM_HALO, SSM_CONV_DIM), F32),
                        pltpu.VMEM((tl, D_INNER), F32),
                        pltpu.VMEM((tl, N_GROUPS * D_STATE), BF16),
                        pltpu.VMEM((tl, N_GROUPS * D_STATE), BF16),
                        pltpu.VMEM((tl, LANES), F32),
                        pltpu.VMEM((tl, D_INNER), F32),
                        pltpu.VMEM((D_INNER, D_STATE), F32)],
        compiler_params=_cparams(2),
        name="mamba_prompt",
    )(h, g, wzx, wdt, dtb, alog, wcv, bcv, dsk, ng, wout)


def _mamba_step_in_kernel(h_ref, g_ref, wzx_ref, wdt_ref, dtb_ref, alog_ref, wcv_ref, bcv_ref,
                          cs_ref, z_ref, act_ref, dt_ref, da_ref, ncs_ref):
    u = _rms(h_ref[...], g_ref[...]).astype(BF16)
    z_ref[...] = _dot(u, wzx_ref[:, :D_INNER])
    xb = _dot(u, wzx_ref[:, D_INNER:])
    dt = _softplus(_dot(u, wdt_ref[...]) + dtb_ref[...])
    dt_ref[...] = dt
    da_ref[...] = jnp.exp(dt * (-jnp.exp(alog_ref[...])))
    nk = SSM_CONV_WIDTH - 1
    cw = SSM_CONV_DIM
    conv = bcv_ref[...] + xb * wcv_ref[nk:nk + 1, :]
    for k in range(nk):
        conv = conv + cs_ref[:, k * cw:(k + 1) * cw] * wcv_ref[k:k + 1, :]
    act_ref[...] = _silu(conv)
    ncs_ref[:, 0:(nk - 1) * cw] = cs_ref[:, cw:nk * cw]
    ncs_ref[:, (nk - 1) * cw:nk * cw] = xb


def _mamba_step_in(h, g, wzx, wdt, dtb, alog, wcv, bcv, cs):
    bsz = h.shape[0]
    cw = (SSM_CONV_WIDTH - 1) * SSM_CONV_DIM
    shapes = [(bsz, D_INNER), (bsz, SSM_CONV_DIM), (bsz, LANES), (bsz, LANES), (bsz, cw)]
    return pl.pallas_call(
        _mamba_step_in_kernel,
        grid=(1,),
        in_specs=[_full((bsz, D_MODEL)), _full((1, D_MODEL)),
                  _full((D_MODEL, D_INNER + SSM_CONV_DIM)), _full((D_MODEL, LANES)),
                  _full((1, LANES)), _full((1, LANES)),
                  _full((SSM_CONV_WIDTH, SSM_CONV_DIM)), _full((1, SSM_CONV_DIM)),
                  _full((bsz, cw))],
        out_specs=[_full(s) for s in shapes],
        out_shape=[jax.ShapeDtypeStruct(s, F32) for s in shapes],
        compiler_params=_cparams(1),
        name="mamba_step_in",
    )(h, g, wzx, wdt, dtb, alog, wcv, bcv, cs)


def _ssm_step_kernel(da_ref, s_ref, act_ref, dt_ref, dsk_ref, *rest, tb):
    ns_ref, y_ref = rest[-2:]
    i = pl.program_id(0)
    rid = lax.broadcasted_iota(jnp.int32, (tb, 1), 0)
    zpad = jnp.zeros((tb, GROUP_W), F32)
    dt = dt_ref[...]
    for g in range(N_GROUPS):
        gs = slice(g * GROUP_W, (g + 1) * GROUP_W)
        xg = act_ref[:, gs]
        bg = act_ref[:, D_INNER + g * D_STATE:D_INNER + (g + 1) * D_STATE]
        cg = act_ref[:, D_INNER + (N_GROUPS + g) * D_STATE:D_INNER + (N_GROUPS + g + 1) * D_STATE]
        xdt = xg * _expand_heads(dt, g, tb)
        y8 = jnp.zeros((tb, GROUP_W), F32)
        for r in range(tb):
            sel = rid == r
            x1 = jnp.concatenate([jnp.where(sel, xdt, 0.0), zpad], axis=0).astype(BF16)
            b1 = jnp.concatenate([jnp.where(sel, bg, 0.0), zpad[:, :D_STATE]], axis=0).astype(BF16)
            c1 = jnp.concatenate([jnp.where(sel, cg, 0.0), zpad[:, :D_STATE]], axis=0).astype(BF16)
            add = _dot_tn(x1, b1)
            news = []
            for k in range(HEADS_PER_GROUP):
                hh = g * HEADS_PER_GROUP + k
                da = da_ref[i * tb + r, hh]
                sn = da * s_ref[r, hh] + add[k * HEAD_DIM:(k + 1) * HEAD_DIM, :]
                ns_ref[r, hh] = sn
                news.append(sn)
            sg = jnp.concatenate(news, axis=0).astype(BF16)
            y8 = y8 + _dot_nt(c1, sg)[:tb, :]
        y_ref[:, gs] = y8 + xg * dsk_ref[:, gs]


def _ssm_step(da, states, layer, act, dt, dsk, tb, prev_out):
    bsz = states.shape[1]
    sblk = pl.BlockSpec((None, tb, N_HEADS, HEAD_DIM, D_STATE), lambda i: (layer, i, 0, 0, 0))
    in_specs = [pl.BlockSpec(memory_space=pltpu.SMEM), sblk,
                pl.BlockSpec((tb, SSM_CONV_DIM), lambda i: (i, 0)),
                pl.BlockSpec((tb, LANES), lambda i: (i, 0)),
                _full((1, D_INNER))]
    args = [da, states, act, dt, dsk]
    aliases = {}
    if prev_out is not None:
        in_specs.append(pl.BlockSpec(memory_space=pl.ANY))
        args.append(prev_out)
        aliases = {len(args) - 1: 0}
    return pl.pallas_call(
        functools.partial(_ssm_step_kernel, tb=tb),
        grid=(bsz // tb,),
        in_specs=in_specs,
        out_specs=[sblk, pl.BlockSpec((tb, D_INNER), lambda i: (i, 0))],
        out_shape=[jax.ShapeDtypeStruct(states.shape, F32),
                   jax.ShapeDtypeStruct((bsz, D_INNER), F32)],
        input_output_aliases=aliases,
        compiler_params=_cparams(1),
        name="ssm_step",
    )(*args)


def _gnorm_out_kernel(y_ref, z_ref, h_ref, ng_ref, wout_ref, o_ref):
    o_ref[...] = _gated_norm_out(y_ref[...], z_ref[...], h_ref[...], ng_ref[...], wout_ref[...])


def _gnorm_out(y, z, h, ng, wout):
    bsz = y.shape[0]
    return pl.pallas_call(
        _gnorm_out_kernel,
        grid=(1,),
        in_specs=[_full((bsz, D_INNER)), _full((bsz, D_INNER)), _full((bsz, D_MODEL)),
                  _full((1, D_INNER)), _full((D_INNER, D_MODEL))],
        out_specs=_full((bsz, D_MODEL)),
        out_shape=jax.ShapeDtypeStruct((bsz, D_MODEL), F32),
        compiler_params=_cparams(1),
        name="gnorm_out",
    )(y, z, h, ng, wout)


def _row(x):
    return x.reshape(1, -1)


def _pad_lanes(x):
    return jnp.pad(x, [(0, 0)] * (x.ndim - 1) + [(0, LANES - x.shape[-1])])


def kernel(x_prompt, x_sample, p_prompt, p_sample, state_conv_mixer, state_ssm, state_ssm_conv, g_mix, g_ffn, g_ple, g_final, cv_w_pw1, cv_b_pw1, cv_w_dw, cv_b_dw, cv_ln_g, cv_ln_b, cv_w_pw2, cv_b_pw2, ssm_w_in, ssm_w_conv, ssm_b_conv, ssm_dt_bias, ssm_a_log, ssm_d, ssm_norm_g, ssm_w_out, ffn_w_gate, ffn_w_up, ffn_w_down, ple_w_proj, ple_w_gate):
    depth = g_mix.shape[0]
    bp, seq, _ = x_prompt.shape
    bs = x_sample.shape[0]
    mp = bp * seq
    tm_prompt = 512
    tl_conv = 512
    tl_ssm = 256
    zx_w = D_INNER + SSM_CONV_DIM

    hp = x_prompt
    hs = x_sample.reshape(bs, D_MODEL)
    conv_p, conv_s, ssm_p, ssmc_p, ssmc_s = [], [], [], [], []
    ssm_s = None
    gfin = _row(g_final)

    for i in range(depth):
        j = i // 2
        gm = _row(g_mix[i])
        if i % 2 == 0:
            w1 = cv_w_pw1[j].astype(BF16)
            b1 = _row(cv_b_pw1[j])
            w2 = cv_w_pw2[j].astype(BF16)
            cargs = (cv_w_dw[j], _row(cv_b_dw[j]), _row(cv_ln_g[j]), _row(cv_ln_b[j]),
                     w2, _row(cv_b_pw2[j]))
            vp = _pw1_glu(hp.reshape(mp, D_MODEL), gm, w1, b1, tm_prompt).reshape(bp, seq, D_MODEL)
            conv_p.append(vp[:, seq - (CONV_WIDTH - 1):, :])
            hp = _conv_prompt(vp, hp, *cargs, tl_conv)
            vs = _pw1_glu(hs, gm, w1, b1, bs)
            st = state_conv_mixer[j].reshape(bs, (CONV_WIDTH - 1) * D_MODEL)
            hs, nst = _conv_step(vs, hs, st, *cargs, 32)
            conv_s.append(nst.reshape(bs, CONV_WIDTH - 1, D_MODEL))
        else:
            w_in = ssm_w_in[j]
            wzx = w_in[:, :zx_w].astype(BF16)
            wdt = _pad_lanes(w_in[:, zx_w:]).astype(BF16)
            dtb = _pad_lanes(_row(ssm_dt_bias[j]))
            alog = _pad_lanes(_row(ssm_a_log[j]))
            wcv = ssm_w_conv[j]
            bcv = _row(ssm_b_conv[j])
            dsk = _row(jnp.repeat(ssm_d[j], HEAD_DIM))
            ng = _row(ssm_norm_g[j])
            wout = ssm_w_out[j].astype(BF16)
            hp, fin, ncs = _mamba_prompt(hp, gm, wzx, wdt, dtb, alog, wcv, bcv, dsk, ng, wout, tl_ssm)
            ssm_p.append(fin.reshape(bp, N_HEADS, HEAD_DIM, D_STATE))
            ssmc_p.append(ncs)
            cs = state_ssm_conv[j].reshape(bs, (SSM_CONV_WIDTH - 1) * SSM_CONV_DIM)
            z, act, dt, da, ncs_s = _mamba_step_in(hs, gm, wzx, wdt, dtb, alog, wcv, bcv, cs)
            ssm_s, y = _ssm_step(da[:, :N_HEADS], state_ssm, j, act, dt, dsk, 8, ssm_s)
            hs = _gnorm_out(y, z, hs, ng, wout)
            ssmc_s.append(ncs_s.reshape(bs, SSM_CONV_WIDTH - 1, SSM_CONV_DIM))
        fargs = (_row(g_ffn[i]), ffn_w_gate[i].astype(BF16), ffn_w_up[i].astype(BF16),
                 ffn_w_down[i].astype(BF16), _row(g_ple[i]), ple_w_gate[i].astype(BF16),
                 ple_w_proj[i].astype(BF16), gfin)
        final = i == depth - 1
        hp = _ffn_ple(hp.reshape(mp, D_MODEL), p_prompt[i].reshape(mp, PLE_DIM), *fargs,
                      tm_prompt, final).reshape(bp, seq, D_MODEL)
        hs = _ffn_ple(hs, p_sample[i].reshape(bs, PLE_DIM), *fargs, bs, final)

    return (hp, hs.reshape(bs, 1, D_MODEL),
            jnp.stack(conv_p), jnp.stack(ssm_p), jnp.stack(ssmc_p),
            jnp.stack(conv_s), ssm_s, jnp.stack(ssmc_s))
```

```python
import functools

import jax
import jax.numpy as jnp
from jax import lax
from jax.experimental import pallas as pl
from jax.experimental.pallas import tpu as pltpu

F32 = jnp.float32
BF16 = jnp.bfloat16

D_MODEL = 1024
CONV_WIDTH = 31
D_INNER = 2048
HEAD_DIM = 64
N_HEADS = 32
N_GROUPS = 8
HEADS_PER_GROUP = 4
D_STATE = 128
SSM_CONV_WIDTH = 4
SSM_CONV_DIM = 4096
D_FF = 2816
PLE_DIM = 256
EPS = 1e-6

LANES = 128
SUBLANES = 8
GROUP_W = HEADS_PER_GROUP * HEAD_DIM
FF_CHUNK = 256
SSD_CHUNK = 128
PERM_ROWS = SSD_CHUNK
PERM_VROWS = PERM_ROWS // SUBLANES
SSM_EDGE = (SSM_CONV_WIDTH - 1) * SUBLANES
VMEM_LIMIT = 60 * 1024 * 1024


def _cparams(n_axes):
    return pltpu.CompilerParams(
        dimension_semantics=("arbitrary",) * n_axes, vmem_limit_bytes=VMEM_LIMIT)


def _sigmoid(x):
    return 1.0 / (1.0 + jnp.exp(-x))


def _silu(x):
    return x * _sigmoid(x)


def _softplus(x):
    return jnp.maximum(x, 0.0) + jnp.log(1.0 + jnp.exp(-jnp.abs(x)))


def _rms(x, g):
    return x * lax.rsqrt(jnp.mean(x * x, axis=-1, keepdims=True) + EPS) * g


def _dot(a, b):
    return jnp.dot(a, b, preferred_element_type=F32)


def _dot_nt(a, b):
    return lax.dot_general(a, b, (((1,), (1,)), ((), ())), preferred_element_type=F32)


def _dot_tn(a, b):
    return lax.dot_general(a, b, (((0,), (0,)), ((), ())), preferred_element_type=F32)


def _split3(x):
    hi = x.astype(BF16)
    r1 = x - hi.astype(F32)
    mid = r1.astype(BF16)
    lo = (r1 - mid.astype(F32)).astype(BF16)
    return hi, mid, lo


def _full(shape):
    return pl.BlockSpec(shape, lambda *_: (0,) * len(shape))


def _time_of_row(shape, axis):
    r = lax.broadcasted_iota(jnp.int32, shape, axis)
    return PERM_VROWS * (r & (SUBLANES - 1)) + lax.shift_right_logical(r, 3)


def _perm_matrix():
    col = lax.broadcasted_iota(jnp.int32, (PERM_ROWS, PERM_ROWS), 1)
    return (col == _time_of_row((PERM_ROWS, PERM_ROWS), 0)).astype(BF16)


def _unperm_matrix():
    row = lax.broadcasted_iota(jnp.int32, (PERM_ROWS, PERM_ROWS), 0)
    return (row == _time_of_row((PERM_ROWS, PERM_ROWS), 1)).astype(BF16)


def _reorder_rows(mat, x):
    blocks = [_dot(mat, x[c:c + PERM_ROWS, :]).astype(BF16) for c in range(0, x.shape[0], PERM_ROWS)]
    return blocks[0] if len(blocks) == 1 else jnp.concatenate(blocks, axis=0)


def _edge_vrow(cur, prv, m, sub):
    s = 1 if m <= PERM_VROWS else 2
    return jnp.where(sub < s, pltpu.roll(prv, s, 0), pltpu.roll(cur, s, 0))


def _pw1_glu_kernel(h_ref, g_ref, w_ref, b_ref, v_ref, *, permute):
    u = _rms(h_ref[...], g_ref[...]).astype(BF16)
    if permute:
        u = _reorder_rows(_perm_matrix(), u)
    a = _dot(u, w_ref[...]) + b_ref[...]
    v_ref[...] = a[:, :D_MODEL] * _sigmoid(a[:, D_MODEL:])


def _pw1_glu(h, g, w, b, tm, permute):
    m = h.shape[0]
    return pl.pallas_call(
        functools.partial(_pw1_glu_kernel, permute=permute),
        grid=(m // tm,),
        in_specs=[pl.BlockSpec((tm, D_MODEL), lambda i: (i, 0)),
                  _full((1, D_MODEL)), _full((D_MODEL, 2 * D_MODEL)), _full((1, 2 * D_MODEL))],
        out_specs=pl.BlockSpec((tm, D_MODEL), lambda i: (i, 0)),
        out_shape=jax.ShapeDtypeStruct((m, D_MODEL), F32),
        compiler_params=_cparams(1),
        name="pw1_glu",
    )(h, g, w, b)


def _ln_silu(c, lng, lnb):
    mu = jnp.mean(c, axis=-1, keepdims=True)
    d = c - mu
    var = jnp.mean(d * d, axis=-1, keepdims=True)
    return _silu(d * lax.rsqrt(var + EPS) * lng + lnb)


def _conv_prompt_kernel(v_ref, h_ref, wdw_ref, bdw_ref, lng_ref, lnb_ref, w2_ref, b2_ref,
                        o_ref, xs, ext, wb, cbuf, *, tl):
    l = pl.program_id(1)
    nb = CONV_WIDTH - 1

    @pl.when((pl.program_id(0) == 0) & (l == 0))
    def _():
        for k in range(CONV_WIDTH):
            wb[k * SUBLANES:(k + 1) * SUBLANES, :] = jnp.broadcast_to(
                wdw_ref[k:k + 1, :], (SUBLANES, D_MODEL))

    @pl.when(l == 0)
    def _():
        xs[0:PERM_ROWS, :] = jnp.zeros((PERM_ROWS, D_MODEL), F32)

    @pl.when(l > 0)
    def _():
        xs[0:PERM_ROWS, :] = xs[tl:tl + PERM_ROWS, :]

    xs[PERM_ROWS:PERM_ROWS + tl, :] = v_ref[...]
    sub = lax.broadcasted_iota(jnp.int32, (SUBLANES, D_MODEL), 0)
    lw = 2 * LANES

    def block(c, carry):
        r0 = pl.multiple_of(c * PERM_ROWS, PERM_ROWS)
        for m in range(1, nb + 1):
            jm = (PERM_VROWS if m <= PERM_VROWS else 2 * PERM_VROWS) - m
            cur = xs[pl.ds(r0 + PERM_ROWS + SUBLANES * jm, SUBLANES), :]
            prv = xs[pl.ds(r0 + SUBLANES * jm, SUBLANES), :]
            ext[(nb - m) * SUBLANES:(nb - m + 1) * SUBLANES, :] = _edge_vrow(cur, prv, m, sub)
        ext[nb * SUBLANES:nb * SUBLANES + PERM_ROWS, :] = xs[pl.ds(r0 + PERM_ROWS, PERM_ROWS), :]
        for lb in range(0, D_MODEL, lw):
            accs = [jnp.broadcast_to(bdw_ref[:, lb:lb + lw], (SUBLANES, lw))] * PERM_VROWS
            for k in range(CONV_WIDTH):
                w8 = wb[k * SUBLANES:(k + 1) * SUBLANES, lb:lb + lw]
                accs = [a + ext[(j + k) * SUBLANES:(j + k + 1) * SUBLANES, lb:lb + lw] * w8
                        for j, a in enumerate(accs)]
            for j, a in enumerate(accs):
                cbuf[pl.ds(r0 + j * SUBLANES, SUBLANES), lb:lb + lw] = a
        return carry

    lax.fori_loop(0, tl // PERM_ROWS, block, 0)
    y = _ln_silu(cbuf[...], lng_ref[...], lnb_ref[...]).astype(BF16)
    y = _reorder_rows(_unperm_matrix(), y)
    o_ref[...] = h_ref[...] + _dot(y, w2_ref[...]) + b2_ref[...]


def _conv_prompt(v, h, wdw, bdw, lng, lnb, w2, b2, tl):
    bsz, seq, _ = v.shape
    tile = pl.BlockSpec((None, tl, D_MODEL), lambda b, l: (b, l, 0))
    return pl.pallas_call(
        functools.partial(_conv_prompt_kernel, tl=tl),
        grid=(bsz, seq // tl),
        in_specs=[tile, tile, _full((CONV_WIDTH, D_MODEL)), _full((1, D_MODEL)),
                  _full((1, D_MODEL)), _full((1, D_MODEL)), _full((D_MODEL, D_MODEL)),
                  _full((1, D_MODEL))],
        out_specs=tile,
        out_shape=jax.ShapeDtypeStruct(v.shape, F32),
        scratch_shapes=[pltpu.VMEM((tl + PERM_ROWS, D_MODEL), F32),
                        pltpu.VMEM(((CONV_WIDTH - 1) * SUBLANES + PERM_ROWS, D_MODEL), F32),
                        pltpu.VMEM((CONV_WIDTH * SUBLANES, D_MODEL), F32),
                        pltpu.VMEM((tl, D_MODEL), F32)],
        compiler_params=_cparams(2),
        name="conv_prompt",
    )(v, h, wdw, bdw, lng, lnb, w2, b2)


def _conv_step_kernel(v_ref, h_ref, st_ref, wdw_ref, bdw_ref, lng_ref, lnb_ref, w2_ref, b2_ref,
                      o_ref, nst_ref):
    v = v_ref[...]
    nk = CONV_WIDTH - 1
    acc = bdw_ref[...] + v * wdw_ref[nk:nk + 1, :]
    for k in range(nk):
        acc = acc + st_ref[:, k * D_MODEL:(k + 1) * D_MODEL] * wdw_ref[k:k + 1, :]
    nst_ref[:, 0:(nk - 1) * D_MODEL] = st_ref[:, D_MODEL:nk * D_MODEL]
    nst_ref[:, (nk - 1) * D_MODEL:nk * D_MODEL] = v
    y = _ln_silu(acc, lng_ref[...], lnb_ref[...]).astype(BF16)
    o_ref[...] = h_ref[...] + _dot(y, w2_ref[...]) + b2_ref[...]


def _conv_step(v, h, st, wdw, bdw, lng, lnb, w2, b2, tb):
    bsz = v.shape[0]
    sw = (CONV_WIDTH - 1) * D_MODEL
    row = pl.BlockSpec((tb, D_MODEL), lambda i: (i, 0))
    strow = pl.BlockSpec((tb, sw), lambda i: (i, 0))
    return pl.pallas_call(
        _conv_step_kernel,
        grid=(bsz // tb,),
        in_specs=[row, row, strow, _full((CONV_WIDTH, D_MODEL)), _full((1, D_MODEL)),
                  _full((1, D_MODEL)), _full((1, D_MODEL)), _full((D_MODEL, D_MODEL)),
                  _full((1, D_MODEL))],
        out_specs=[row, strow],
        out_shape=[jax.ShapeDtypeStruct((bsz, D_MODEL), F32),
                   jax.ShapeDtypeStruct((bsz, sw), F32)],
        compiler_params=_cparams(1),
        name="conv_step",
    )(v, h, st, wdw, bdw, lng, lnb, w2, b2)


def _ffn_ple_kernel(h_ref, p_ref, gf_ref, wg_ref, wu_ref, wd_ref, gp_ref, wpg_ref, wpp_ref,
                    gfin_ref, o_ref, *, final):
    h = h_ref[...]
    u = _rms(h, gf_ref[...]).astype(BF16)
    acc = h
    for f in range(0, D_FF, FF_CHUNK):
        gate = _dot(u, wg_ref[:, f:f + FF_CHUNK])
        up = _dot(u, wu_ref[:, f:f + FF_CHUNK])
        a = (_silu(gate) * up).astype(BF16)
        acc = acc + _dot(a, wd_ref[f:f + FF_CHUNK, :])
    pg = _sigmoid(_dot(_rms(acc, gp_ref[...]).astype(BF16), wpg_ref[...]))
    out = acc + _dot(p_ref[...].astype(BF16), wpp_ref[...]) * pg
    if final:
        out = _rms(out, gfin_ref[...])
    o_ref[...] = out


def _ffn_ple(h, p, gf, wg, wu, wd, gp, wpg, wpp, gfin, tm, final):
    m = h.shape[0]
    return pl.pallas_call(
        functools.partial(_ffn_ple_kernel, final=final),
        grid=(m // tm,),
        in_specs=[pl.BlockSpec((tm, D_MODEL), lambda i: (i, 0)),
                  pl.BlockSpec((tm, PLE_DIM), lambda i: (i, 0)),
                  _full((1, D_MODEL)), _full((D_MODEL, D_FF)), _full((D_MODEL, D_FF)),
                  _full((D_FF, D_MODEL)), _full((1, D_MODEL)), _full((D_MODEL, D_MODEL)),
                  _full((PLE_DIM, D_MODEL)), _full((1, D_MODEL))],
        out_specs=pl.BlockSpec((tm, D_MODEL), lambda i: (i, 0)),
        out_shape=jax.ShapeDtypeStruct((m, D_MODEL), F32),
        compiler_params=_cparams(1),
        name="ffn_ple",
    )(h, p, gf, wg, wu, wd, gp, wpg, wpp, gfin)


def _gated_norm(y, z, ng):
    yg = y * _silu(z)
    parts = []
    for g in range(N_GROUPS):
        blk = yg[:, g * GROUP_W:(g + 1) * GROUP_W]
        ms = jnp.mean(blk * blk, axis=-1, keepdims=True)
        parts.append(blk * lax.rsqrt(ms + EPS))
    return (jnp.concatenate(parts, axis=-1) * ng).astype(BF16)


def _lane_block_ids(shape, width):
    return lax.broadcasted_iota(jnp.int32, shape, 1) // width


def _expand_heads(v, g, rows):
    blk = _lane_block_ids((rows, GROUP_W), HEAD_DIM)
    out = jnp.zeros((rows, GROUP_W), F32)
    for r in range(HEADS_PER_GROUP):
        hh = g * HEADS_PER_GROUP + r
        col = jnp.broadcast_to(v[:, hh:hh + 1], (rows, GROUP_W))
        out = jnp.where(blk == r, col, out)
    return out


def _mamba_prompt_kernel(h_ref, g_ref, wzx_ref, wdt_ref, dtb_ref, alog_ref, wcv_ref, bcv_ref,
                         dsk_ref, ng_ref, wout_ref,
                         o_ref, st_ref, cs_ref,
                         z_scr, xb_scr, x_scr, b_scr, c_scr, dt_scr, y_scr, s_scr, *, tl):
    l = pl.program_id(1)
    q = SSD_CHUNK
    nk = SSM_CONV_WIDTH - 1

    @pl.when(l == 0)
    def _():
        s_scr[...] = jnp.zeros(s_scr.shape, F32)
        xb_scr[0:SSM_EDGE, :] = jnp.zeros((SSM_EDGE, SSM_CONV_DIM), F32)

    @pl.when(l > 0)
    def _():
        xb_scr[0:SSM_EDGE, :] = xb_scr[tl:tl + SSM_EDGE, :]

    h = h_ref[...]
    u = _reorder_rows(_perm_matrix(), _rms(h, g_ref[...]).astype(BF16))
    z_scr[...] = _dot(u, wzx_ref[:, :D_INNER])
    xb_scr[SSM_EDGE:SSM_EDGE + tl, :] = _dot(u, wzx_ref[:, D_INNER:])
    dt_scr[...] = _softplus(_dot(u, wdt_ref[...]) + dtb_ref[...])

    for k in range(nk):
        r = tl + SSM_EDGE - (nk - k) * SUBLANES + SUBLANES - 1
        cs_ref[k:k + 1, :] = xb_scr[r:r + 1, :]

    cw = 4 * LANES
    sub = lax.broadcasted_iota(jnp.int32, (SUBLANES, cw), 0)
    for cb0 in range(0, SSM_CONV_DIM, cw):
        cl = slice(cb0, cb0 + cw)
        for c in range(tl // q):
            base = c * q
            cur = xb_scr[base + SSM_EDGE:base + SSM_EDGE + q, cl]
            edges = []
            for m in range(nk, 0, -1):
                jm = PERM_VROWS - m
                cur_v = cur[jm * SUBLANES:(jm + 1) * SUBLANES, :]
                prv_v = xb_scr[base + (nk - m) * SUBLANES:base + (nk - m + 1) * SUBLANES, cl]
                edges.append(_edge_vrow(cur_v, prv_v, m, sub))
            hist = jnp.concatenate(edges + [cur], axis=0)
            conv = bcv_ref[:, cl] + cur * wcv_ref[nk:nk + 1, cl]
            for s in range(1, SSM_CONV_WIDTH):
                lo = SSM_EDGE - s * SUBLANES
                conv = conv + hist[lo:lo + q, :] * wcv_ref[nk - s:nk - s + 1, cl]
            act = _silu(conv)
            rows = slice(base, base + q)
            if cb0 < D_INNER:
                x_scr[rows, cl] = act
            elif cb0 < D_INNER + N_GROUPS * D_STATE:
                b_scr[rows, cb0 - D_INNER:cb0 - D_INNER + cw] = act.astype(BF16)
            else:
                c0 = cb0 - D_INNER - N_GROUPS * D_STATE
                c_scr[rows, c0:c0 + cw] = act.astype(BF16)

    a_neg = -jnp.exp(alog_ref[...])
    causal = _time_of_row((q, q), 0) >= _time_of_row((q, q), 1)
    tril = causal.astype(BF16)
    low_half = lax.broadcasted_iota(jnp.int32, (q, LANES), 1) < HEAD_DIM
    blk = _lane_block_ids((q, GROUP_W), HEAD_DIM)

    def pair_expand(cols):
        return jnp.concatenate([jnp.where(low_half, cols[0], cols[1]),
                                jnp.where(low_half, cols[2], cols[3])], axis=-1)

    for c in range(tl // q):
        rows = slice(c * q, (c + 1) * q)
        dtc = dt_scr[rows, :]
        a = dtc * a_neg
        a_hi, a_mid, a_lo = _split3(a)
        a_cum = _dot(tril, a_hi) + _dot(tril, a_mid) + _dot(tril, a_lo)
        a_cum_t = a_cum.T
        for g in range(N_GROUPS):
            gs = slice(g * GROUP_W, (g + 1) * GROUP_W)
            ns = slice(g * D_STATE, (g + 1) * D_STATE)
            heads = [g * HEADS_PER_GROUP + r for r in range(HEADS_PER_GROUP)]
            bg = b_scr[rows, ns]
            cg = c_scr[rows, ns]
            xg = x_scr[rows, gs]
            sg = s_scr[gs, :]
            cb = _dot_nt(cg, bg)
            y_off = _dot_nt(cg, sg.astype(BF16))
            bc_a = [jnp.broadcast_to(a_cum[:, hh:hh + 1], (q, LANES)) for hh in heads]
            bc_dt = [jnp.broadcast_to(dtc[:, hh:hh + 1], (q, LANES)) for hh in heads]
            a_e = pair_expand(bc_a)
            xdt = xg * pair_expand(bc_dt)
            ms = []
            for r, hh in enumerate(heads):
                seg = bc_a[r] - a_cum_t[hh:hh + 1, :]
                lm = jnp.exp(jnp.where(causal, seg, -jnp.inf))
                ms.append((cb * lm).astype(BF16))
            xblk = jnp.concatenate(
                [jnp.where(blk == r, xdt, 0.0).astype(BF16) for r in range(HEADS_PER_GROUP)], axis=0)
            y = (_dot(jnp.concatenate(ms, axis=-1), xblk) + y_off * jnp.exp(a_e)
                 + xg * dsk_ref[:, gs])
            y_scr[rows, gs] = y
            xs = (xdt * jnp.exp(a_e[q - 1:q, :] - a_e)).astype(BF16)
            add = _dot_tn(xs, bg)
            for r, hh in enumerate(heads):
                cd = jnp.exp(bc_a[r][q - 1:q, :])
                hs = slice(hh * HEAD_DIM, (hh + 1) * HEAD_DIM)
                s_scr[hs, :] = cd * s_scr[hs, :] + add[r * HEAD_DIM:(r + 1) * HEAD_DIM, :]

    st_ref[...] = s_scr[...]
    yn = _reorder_rows(_unperm_matrix(), _gated_norm(y_scr[...], z_scr[...], ng_ref[...]))
    o_ref[...] = h + _dot(yn, wout_ref[...])


def _mamba_prompt(h, g, wzx, wdt, dtb, alog, wcv, bcv, dsk, ng, wout, tl):
    bsz, seq, _ = h.shape
    tile = pl.BlockSpec((None, tl, D_MODEL), lambda b, l: (b, l, 0))
    nconv = SSM_CONV_WIDTH - 1
    return pl.pallas_call(
        functools.partial(_mamba_prompt_kernel, tl=tl),
        grid=(bsz, seq // tl),
        in_specs=[tile, _full((1, D_MODEL)), _full((D_MODEL, D_INNER + SSM_CONV_DIM)),
                  _full((D_MODEL, LANES)), _full((1, LANES)), _full((1, LANES)),
                  _full((SSM_CONV_WIDTH, SSM_CONV_DIM)), _full((1, SSM_CONV_DIM)),
                  _full((1, D_INNER)), _full((1, D_INNER)), _full((D_INNER, D_MODEL))],
        out_specs=[tile,
                   pl.BlockSpec((None, D_INNER, D_STATE), lambda b, l: (b, 0, 0)),
                   pl.BlockSpec((None, nconv, SSM_CONV_DIM), lambda b, l: (b, 0, 0))],
        out_shape=[jax.ShapeDtypeStruct(h.shape, F32),
                   jax.ShapeDtypeStruct((bsz, D_INNER, D_STATE), F32),
                   jax.ShapeDtypeStruct((bsz, nconv, SSM_CONV_DIM), F32)],
        scratch_shapes=[pltpu.VMEM((tl, D_INNER), F32),
                        pltpu.VMEM((tl + SSM_EDGE, SSM_CONV_DIM), F32),
                        pltpu.VMEM((tl, D_INNER), F32),
                        pltpu.VMEM((tl, N_GROUPS * D_STATE), BF16),
                        pltpu.VMEM((tl, N_GROUPS * D_STATE), BF16),
                        pltpu.VMEM((tl, LANES), F32),
                        pltpu.VMEM((tl, D_INNER), F32),
                        pltpu.VMEM((D_INNER, D_STATE), F32)],
        compiler_params=_cparams(2),
        name="mamba_prompt",
    )(h, g, wzx, wdt, dtb, alog, wcv, bcv, dsk, ng, wout)


def _mamba_step_in_kernel(h_ref, g_ref, wzx_ref, wdt_ref, dtb_ref, alog_ref, wcv_ref, bcv_ref,
                          cs_ref, z_ref, act_ref, dt_ref, da_ref, ncs_ref):
    u = _rms(h_ref[...], g_ref[...]).astype(BF16)
    z_ref[...] = _dot(u, wzx_ref[:, :D_INNER])
    xb = _dot(u, wzx_ref[:, D_INNER:])
    dt = _softplus(_dot(u, wdt_ref[...]) + dtb_ref[...])
    dt_ref[...] = dt
    da_ref[...] = jnp.exp(dt * (-jnp.exp(alog_ref[...])))
    nk = SSM_CONV_WIDTH - 1
    cw = SSM_CONV_DIM
    conv = bcv_ref[...] + xb * wcv_ref[nk:nk + 1, :]
    for k in range(nk):
        conv = conv + cs_ref[:, k * cw:(k + 1) * cw] * wcv_ref[k:k + 1, :]
    act_ref[...] = _silu(conv)
    ncs_ref[:, 0:(nk - 1) * cw] = cs_ref[:, cw:nk * cw]
    ncs_ref[:, (nk - 1) * cw:nk * cw] = xb


def _mamba_step_in(h, g, wzx, wdt, dtb, alog, wcv, bcv, cs):
    bsz = h.shape[0]
    cw = (SSM_CONV_WIDTH - 1) * SSM_CONV_DIM
    shapes = [(bsz, D_INNER), (bsz, SSM_CONV_DIM), (bsz, LANES), (bsz, LANES), (bsz, cw)]
    return pl.pallas_call(
        _mamba_step_in_kernel,
        grid=(1,),
        in_specs=[_full((bsz, D_MODEL)), _full((1, D_MODEL)),
                  _full((D_MODEL, D_INNER + SSM_CONV_DIM)), _full((D_MODEL, LANES)),
                  _full((1, LANES)), _full((1, LANES)),
                  _full((SSM_CONV_WIDTH, SSM_CONV_DIM)), _full((1, SSM_CONV_DIM)),
                  _full((bsz, cw))],
        out_specs=[_full(s) for s in shapes],
        out_shape=[jax.ShapeDtypeStruct(s, F32) for s in shapes],
        compiler_params=_cparams(1),
        name="mamba_step_in",
    )(h, g, wzx, wdt, dtb, alog, wcv, bcv, cs)


def _ssm_step_kernel(da_ref, s_ref, act_ref, dt_ref, dsk_ref, *rest, tb):
    ns_ref, y_ref = rest[-2:]
    i = pl.program_id(0)
    rid = lax.broadcasted_iota(jnp.int32, (tb, 1), 0)
    zpad = jnp.zeros((tb, GROUP_W), F32)
    dt = dt_ref[...]
    for g in range(N_GROUPS):
        gs = slice(g * GROUP_W, (g + 1) * GROUP_W)
        xg = act_ref[:, gs]
        bg = act_ref[:, D_INNER + g * D_STATE:D_INNER + (g + 1) * D_STATE]
        cg = act_ref[:, D_INNER + (N_GROUPS + g) * D_STATE:D_INNER + (N_GROUPS + g + 1) * D_STATE]
        xdt = xg * _expand_heads(dt, g, tb)
        y8 = jnp.zeros((tb, GROUP_W), F32)
        for r in range(tb):
            sel = rid == r
            x1 = jnp.concatenate([jnp.where(sel, xdt, 0.0), zpad], axis=0).astype(BF16)
            b1 = jnp.concatenate([jnp.where(sel, bg, 0.0), zpad[:, :D_STATE]], axis=0).astype(BF16)
            c1 = jnp.concatenate([jnp.where(sel, cg, 0.0), zpad[:, :D_STATE]], axis=0).astype(BF16)
            add = _dot_tn(x1, b1)
            news = []
            for k in range(HEADS_PER_GROUP):
                hh = g * HEADS_PER_GROUP + k
                da = da_ref[i * tb + r, hh]
                sn = da * s_ref[r, hh] + add[k * HEAD_DIM:(k + 1) * HEAD_DIM, :]
                ns_ref[r, hh] = sn
                news.append(sn)
            sg = jnp.concatenate(news, axis=0).astype(BF16)
            y8 = y8 + _dot_nt(c1, sg)[:tb, :]
        y_ref[:, gs] = y8 + xg * dsk_ref[:, gs]


def _ssm_step(da, states, layer, act, dt, dsk, tb, prev_out):
    bsz = states.shape[1]
    sblk = pl.BlockSpec((None, tb, N_HEADS, HEAD_DIM, D_STATE), lambda i: (layer, i, 0, 0, 0))
    in_specs = [pl.BlockSpec(memory_space=pltpu.SMEM), sblk,
                pl.BlockSpec((tb, SSM_CONV_DIM), lambda i: (i, 0)),
                pl.BlockSpec((tb, LANES), lambda i: (i, 0)),
                _full((1, D_INNER))]
    args = [da, states, act, dt, dsk]
    aliases = {}
    if prev_out is not None:
        in_specs.append(pl.BlockSpec(memory_space=pl.ANY))
        args.append(prev_out)
        aliases = {len(args) - 1: 0}
    return pl.pallas_call(
        functools.partial(_ssm_step_kernel, tb=tb),
        grid=(bsz // tb,),
        in_specs=in_specs,
        out_specs=[sblk, pl.BlockSpec((tb, D_INNER), lambda i: (i, 0))],
        out_shape=[jax.ShapeDtypeStruct(states.shape, F32),
                   jax.ShapeDtypeStruct((bsz, D_INNER), F32)],
        input_output_aliases=aliases,
        compiler_params=_cparams(1),
        name="ssm_step",
    )(*args)


def _gnorm_out_kernel(y_ref, z_ref, h_ref, ng_ref, wout_ref, o_ref):
    o_ref[...] = h_ref[...] + _dot(_gated_norm(y_ref[...], z_ref[...], ng_ref[...]), wout_ref[...])


def _gnorm_out(y, z, h, ng, wout):
    bsz = y.shape[0]
    return pl.pallas_call(
        _gnorm_out_kernel,
        grid=(1,),
        in_specs=[_full((bsz, D_INNER)), _full((bsz, D_INNER)), _full((bsz, D_MODEL)),
                  _full((1, D_INNER)), _full((D_INNER, D_MODEL))],
        out_specs=_full((bsz, D_MODEL)),
        out_shape=jax.ShapeDtypeStruct((bsz, D_MODEL), F32),
        compiler_params=_cparams(1),
        name="gnorm_out",
    )(y, z, h, ng, wout)


def _row(x):
    return x.reshape(1, -1)


def _pad_lanes(x):
    return jnp.pad(x, [(0, 0)] * (x.ndim - 1) + [(0, LANES - x.shape[-1])])


def kernel(x_prompt, x_sample, p_prompt, p_sample, state_conv_mixer, state_ssm, state_ssm_conv, g_mix, g_ffn, g_ple, g_final, cv_w_pw1, cv_b_pw1, cv_w_dw, cv_b_dw, cv_ln_g, cv_ln_b, cv_w_pw2, cv_b_pw2, ssm_w_in, ssm_w_conv, ssm_b_conv, ssm_dt_bias, ssm_a_log, ssm_d, ssm_norm_g, ssm_w_out, ffn_w_gate, ffn_w_up, ffn_w_down, ple_w_proj, ple_w_gate):
    depth = g_mix.shape[0]
    bp, seq, _ = x_prompt.shape
    bs = x_sample.shape[0]
    mp = bp * seq
    tm_prompt = 512
    tl_conv = 512
    tl_ssm = 256
    zx_w = D_INNER + SSM_CONV_DIM

    hp = x_prompt
    hs = x_sample.reshape(bs, D_MODEL)
    conv_p, conv_s, ssm_p, ssmc_p, ssmc_s = [], [], [], [], []
    ssm_s = None
    gfin = _row(g_final)

    for i in range(depth):
        j = i // 2
        gm = _row(g_mix[i])
        if i % 2 == 0:
            w1 = cv_w_pw1[j].astype(BF16)
            b1 = _row(cv_b_pw1[j])
            w2 = cv_w_pw2[j].astype(BF16)
            cargs = (cv_w_dw[j], _row(cv_b_dw[j]), _row(cv_ln_g[j]), _row(cv_ln_b[j]),
                     w2, _row(cv_b_pw2[j]))
            vp = _pw1_glu(hp.reshape(mp, D_MODEL), gm, w1, b1, tm_prompt, True).reshape(bp, seq, D_MODEL)
            tail = vp[:, seq - PERM_ROWS:, :].reshape(bp, PERM_VROWS, SUBLANES, D_MODEL)
            tail = jnp.swapaxes(tail, 1, 2).reshape(bp, PERM_ROWS, D_MODEL)
            conv_p.append(tail[:, PERM_ROWS - (CONV_WIDTH - 1):, :])
            hp = _conv_prompt(vp, hp, *cargs, tl_conv)
            vs = _pw1_glu(hs, gm, w1, b1, bs, False)
            st = state_conv_mixer[j].reshape(bs, (CONV_WIDTH - 1) * D_MODEL)
            hs, nst = _conv_step(vs, hs, st, *cargs, 32)
            conv_s.append(nst.reshape(bs, CONV_WIDTH - 1, D_MODEL))
        else:
            w_in = ssm_w_in[j]
            wzx = w_in[:, :zx_w].astype(BF16)
            wdt = _pad_lanes(w_in[:, zx_w:]).astype(BF16)
            dtb = _pad_lanes(_row(ssm_dt_bias[j]))
            alog = _pad_lanes(_row(ssm_a_log[j]))
            wcv = ssm_w_conv[j]
            bcv = _row(ssm_b_conv[j])
            dsk = _row(jnp.repeat(ssm_d[j], HEAD_DIM))
            ng = _row(ssm_norm_g[j])
            wout = ssm_w_out[j].astype(BF16)
            hp, fin, ncs = _mamba_prompt(hp, gm, wzx, wdt, dtb, alog, wcv, bcv, dsk, ng, wout, tl_ssm)
            ssm_p.append(fin.reshape(bp, N_HEADS, HEAD_DIM, D_STATE))
            ssmc_p.append(ncs)
            cs = state_ssm_conv[j].reshape(bs, (SSM_CONV_WIDTH - 1) * SSM_CONV_DIM)
            z, act, dt, da, ncs_s = _mamba_step_in(hs, gm, wzx, wdt, dtb, alog, wcv, bcv, cs)
            ssm_s, y = _ssm_step(da[:, :N_HEADS], state_ssm, j, act, dt, dsk, 8, ssm_s)
            hs = _gnorm_out(y, z, hs, ng, wout)
            ssmc_s.append(ncs_s.reshape(bs, SSM_CONV_WIDTH - 1, SSM_CONV_DIM))
        fargs = (_row(g_ffn[i]), ffn_w_gate[i].astype(BF16), ffn_w_up[i].astype(BF16),
                 ffn_w_down[i].astype(BF16), _row(g_ple[i]), ple_w_gate[i].astype(BF16),
                 ple_w_proj[i].astype(BF16), gfin)
        final = i == depth - 1
        hp = _ffn_ple(hp.reshape(mp, D_MODEL), p_prompt[i].reshape(mp, PLE_DIM), *fargs,
                      tm_prompt, final).reshape(bp, seq, D_MODEL)
        hs = _ffn_ple(hs, p_sample[i].reshape(bs, PLE_DIM), *fargs, bs, final)

    return (hp, hs.reshape(bs, 1, D_MODEL),
            jnp.stack(conv_p), jnp.stack(ssm_p), jnp.stack(ssmc_p),
            jnp.stack(conv_s), ssm_s, jnp.stack(ssmc_s))
```

```python
import functools

import jax
import jax.numpy as jnp
from jax import lax
from jax.experimental import pallas as pl
from jax.experimental.pallas import tpu as pltpu

F32 = jnp.float32
BF16 = jnp.bfloat16

D_MODEL = 1024
CONV_WIDTH = 31
D_INNER = 2048
HEAD_DIM = 64
N_HEADS = 32
N_GROUPS = 8
HEADS_PER_GROUP = 4
D_STATE = 128
SSM_CONV_WIDTH = 4
SSM_CONV_DIM = 4096
D_FF = 2816
PLE_DIM = 256
EPS = 1e-6

LANES = 128
SUBLANES = 8
GROUP_W = HEADS_PER_GROUP * HEAD_DIM
FF_CHUNK = 256
SSD_CHUNK = 128
PERM_ROWS = SSD_CHUNK
PERM_VROWS = PERM_ROWS // SUBLANES
SSM_EDGE = (SSM_CONV_WIDTH - 1) * SUBLANES
VMEM_LIMIT = 60 * 1024 * 1024


def _cparams(n_axes):
    return pltpu.CompilerParams(
        dimension_semantics=("arbitrary",) * n_axes, vmem_limit_bytes=VMEM_LIMIT)


def _sigmoid(x):
    return 1.0 / (1.0 + jnp.exp(-x))


def _silu(x):
    return x * _sigmoid(x)


def _softplus(x):
    return jnp.maximum(x, 0.0) + jnp.log(1.0 + jnp.exp(-jnp.abs(x)))


def _rms(x, g):
    return x * lax.rsqrt(jnp.mean(x * x, axis=-1, keepdims=True) + EPS) * g


def _dot(a, b):
    return jnp.dot(a, b, preferred_element_type=F32)


def _dot_nt(a, b):
    return lax.dot_general(a, b, (((1,), (1,)), ((), ())), preferred_element_type=F32)


def _dot_tn(a, b):
    return lax.dot_general(a, b, (((0,), (0,)), ((), ())), preferred_element_type=F32)


def _split3(x):
    hi = x.astype(BF16)
    r1 = x - hi.astype(F32)
    mid = r1.astype(BF16)
    lo = (r1 - mid.astype(F32)).astype(BF16)
    return hi, mid, lo


def _full(shape):
    return pl.BlockSpec(shape, lambda *_: (0,) * len(shape))


def _time_of_row(shape, axis):
    r = lax.broadcasted_iota(jnp.int32, shape, axis)
    return PERM_VROWS * (r & (SUBLANES - 1)) + lax.shift_right_logical(r, 3)


def _perm_matrix():
    col = lax.broadcasted_iota(jnp.int32, (PERM_ROWS, PERM_ROWS), 1)
    return (col == _time_of_row((PERM_ROWS, PERM_ROWS), 0)).astype(BF16)


def _unperm_matrix():
    row = lax.broadcasted_iota(jnp.int32, (PERM_ROWS, PERM_ROWS), 0)
    return (row == _time_of_row((PERM_ROWS, PERM_ROWS), 1)).astype(BF16)


def _reorder_rows(mat, x):
    blocks = [_dot(mat, x[c:c + PERM_ROWS, :]).astype(BF16) for c in range(0, x.shape[0], PERM_ROWS)]
    return blocks[0] if len(blocks) == 1 else jnp.concatenate(blocks, axis=0)


def _edge_vrow(cur, prv, m, sub):
    s = 1 if m <= PERM_VROWS else 2
    return jnp.where(sub < s, pltpu.roll(prv, s, 0), pltpu.roll(cur, s, 0))


def _pw1_glu_kernel(h_ref, g_ref, w_ref, b_ref, v_ref, *, permute):
    u = _rms(h_ref[...], g_ref[...]).astype(BF16)
    if permute:
        u = _reorder_rows(_perm_matrix(), u)
    a = _dot(u, w_ref[...]) + b_ref[...]
    v_ref[...] = a[:, :D_MODEL] * _sigmoid(a[:, D_MODEL:])


def _pw1_glu(h, g, w, b, tm, permute):
    m = h.shape[0]
    return pl.pallas_call(
        functools.partial(_pw1_glu_kernel, permute=permute),
        grid=(m // tm,),
        in_specs=[pl.BlockSpec((tm, D_MODEL), lambda i: (i, 0)),
                  _full((1, D_MODEL)), _full((D_MODEL, 2 * D_MODEL)), _full((1, 2 * D_MODEL))],
        out_specs=pl.BlockSpec((tm, D_MODEL), lambda i: (i, 0)),
        out_shape=jax.ShapeDtypeStruct((m, D_MODEL), F32),
        compiler_params=_cparams(1),
        name="pw1_glu",
    )(h, g, w, b)


def _ln_silu(c, lng, lnb):
    mu = jnp.mean(c, axis=-1, keepdims=True)
    d = c - mu
    var = jnp.mean(d * d, axis=-1, keepdims=True)
    return _silu(d * lax.rsqrt(var + EPS) * lng + lnb)


def _conv_prompt_kernel(v_ref, h_ref, wdw_ref, bdw_ref, lng_ref, lnb_ref, w2_ref, b2_ref,
                        o_ref, xs, ext, wb, cbuf, *, tl):
    l = pl.program_id(1)
    nb = CONV_WIDTH - 1

    @pl.when((pl.program_id(0) == 0) & (l == 0))
    def _():
        for k in range(CONV_WIDTH):
            wb[k * SUBLANES:(k + 1) * SUBLANES, :] = jnp.broadcast_to(
                wdw_ref[k:k + 1, :], (SUBLANES, D_MODEL))

    @pl.when(l == 0)
    def _():
        xs[0:PERM_ROWS, :] = jnp.zeros((PERM_ROWS, D_MODEL), F32)

    @pl.when(l > 0)
    def _():
        xs[0:PERM_ROWS, :] = xs[tl:tl + PERM_ROWS, :]

    xs[PERM_ROWS:PERM_ROWS + tl, :] = v_ref[...]
    sub = lax.broadcasted_iota(jnp.int32, (SUBLANES, D_MODEL), 0)
    lw = 2 * LANES

    def block(c, carry):
        r0 = pl.multiple_of(c * PERM_ROWS, PERM_ROWS)
        for m in range(1, nb + 1):
            jm = (PERM_VROWS if m <= PERM_VROWS else 2 * PERM_VROWS) - m
            cur = xs[pl.ds(r0 + PERM_ROWS + SUBLANES * jm, SUBLANES), :]
            prv = xs[pl.ds(r0 + SUBLANES * jm, SUBLANES), :]
            ext[(nb - m) * SUBLANES:(nb - m + 1) * SUBLANES, :] = _edge_vrow(cur, prv, m, sub)
        ext[nb * SUBLANES:nb * SUBLANES + PERM_ROWS, :] = xs[pl.ds(r0 + PERM_ROWS, PERM_ROWS), :]
        for lb in range(0, D_MODEL, lw):
            accs = [jnp.broadcast_to(bdw_ref[:, lb:lb + lw], (SUBLANES, lw))] * PERM_VROWS
            for k in range(CONV_WIDTH):
                w8 = wb[k * SUBLANES:(k + 1) * SUBLANES, lb:lb + lw]
                accs = [a + ext[(j + k) * SUBLANES:(j + k + 1) * SUBLANES, lb:lb + lw] * w8
                        for j, a in enumerate(accs)]
            for j, a in enumerate(accs):
                cbuf[pl.ds(r0 + j * SUBLANES, SUBLANES), lb:lb + lw] = a
        return carry

    lax.fori_loop(0, tl // PERM_ROWS, block, 0)
    y = _ln_silu(cbuf[...], lng_ref[...], lnb_ref[...]).astype(BF16)
    y = _reorder_rows(_unperm_matrix(), y)
    o_ref[...] = h_ref[...] + _dot(y, w2_ref[...]) + b2_ref[...]


def _conv_prompt(v, h, wdw, bdw, lng, lnb, w2, b2, tl):
    bsz, seq, _ = v.shape
    tile = pl.BlockSpec((None, tl, D_MODEL), lambda b, l: (b, l, 0))
    return pl.pallas_call(
        functools.partial(_conv_prompt_kernel, tl=tl),
        grid=(bsz, seq // tl),
        in_specs=[tile, tile, _full((CONV_WIDTH, D_MODEL)), _full((1, D_MODEL)),
                  _full((1, D_MODEL)), _full((1, D_MODEL)), _full((D_MODEL, D_MODEL)),
                  _full((1, D_MODEL))],
        out_specs=tile,
        out_shape=jax.ShapeDtypeStruct(v.shape, F32),
        scratch_shapes=[pltpu.VMEM((tl + PERM_ROWS, D_MODEL), F32),
                        pltpu.VMEM(((CONV_WIDTH - 1) * SUBLANES + PERM_ROWS, D_MODEL), F32),
                        pltpu.VMEM((CONV_WIDTH * SUBLANES, D_MODEL), F32),
                        pltpu.VMEM((tl, D_MODEL), F32)],
        compiler_params=_cparams(2),
        name="conv_prompt",
    )(v, h, wdw, bdw, lng, lnb, w2, b2)


def _stack_plan(layer, n_layers, nb, fill):
    park = 0 if layer > 0 else nb - 1
    if fill:
        return (n_layers, nb), (lambda ls, i: jnp.where(ls == layer, i, park)), (lambda ls: ls)
    return (1, nb), (lambda ls, i: i), (lambda ls: layer)


def _own_slot_or_zero(fill, layer, state_out_ref, body):
    if not fill:
        body()
        return

    @pl.when(pl.program_id(0) != layer)
    def _():
        state_out_ref[...] = jnp.zeros(state_out_ref.shape, F32)

    pl.when(pl.program_id(0) == layer)(body)


def _conv_step_kernel(v_ref, h_ref, st_ref, wdw_ref, bdw_ref, lng_ref, lnb_ref, w2_ref, b2_ref,
                      *rest, tb, layer, fill):
    o_ref, nst_ref, acc_scr = rest[-3:]
    nk = CONV_WIDTH - 1

    def body():
        w_hist = wdw_ref[0:nk, :]
        for r in range(tb):
            blk = st_ref[r]
            acc_scr[r:r + 1, :] = jnp.sum(blk * w_hist, axis=0, keepdims=True)
            nst_ref[r, 0:nk - 1, :] = blk[1:nk, :]
            nst_ref[r, nk - 1:nk, :] = v_ref[r:r + 1, :]
        acc = acc_scr[...] + bdw_ref[...] + v_ref[...] * wdw_ref[nk:nk + 1, :]
        y = _ln_silu(acc, lng_ref[...], lnb_ref[...]).astype(BF16)
        o_ref[...] = h_ref[...] + _dot(y, w2_ref[...]) + b2_ref[...]

    _own_slot_or_zero(fill, layer, nst_ref, body)


def _conv_step(v, h, states, layer, wdw, bdw, lng, lnb, w2, b2, tb, prev_out):
    n_layers, bsz = states.shape[:2]
    fill = prev_out is None
    grid, bi, li = _stack_plan(layer, n_layers, bsz // tb, fill)
    row = pl.BlockSpec((tb, D_MODEL), lambda ls, i: (bi(ls, i), 0))
    sblk = (None, tb, CONV_WIDTH - 1, D_MODEL)
    in_specs = [row, row, pl.BlockSpec(sblk, lambda ls, i: (layer, bi(ls, i), 0, 0)),
                _full((CONV_WIDTH, D_MODEL)), _full((1, D_MODEL)),
                _full((1, D_MODEL)), _full((1, D_MODEL)), _full((D_MODEL, D_MODEL)),
                _full((1, D_MODEL))]
    args = [v, h, states, wdw, bdw, lng, lnb, w2, b2]
    aliases = {}
    if not fill:
        in_specs.append(pl.BlockSpec(memory_space=pl.ANY))
        args.append(prev_out)
        aliases = {len(args) - 1: 1}
    return pl.pallas_call(
        functools.partial(_conv_step_kernel, tb=tb, layer=layer, fill=fill),
        grid=grid,
        in_specs=in_specs,
        out_specs=[row, pl.BlockSpec(sblk, lambda ls, i: (li(ls), i, 0, 0))],
        out_shape=[jax.ShapeDtypeStruct((bsz, D_MODEL), F32),
                   jax.ShapeDtypeStruct(states.shape, F32)],
        scratch_shapes=[pltpu.VMEM((tb, D_MODEL), F32)],
        input_output_aliases=aliases,
        compiler_params=_cparams(2),
        name="conv_step",
    )(*args)


def _ffn_ple_kernel(h_ref, p_ref, gf_ref, wg_ref, wu_ref, wd_ref, gp_ref, wpg_ref, wpp_ref,
                    gfin_ref, o_ref, *, final):
    h = h_ref[...]
    u = _rms(h, gf_ref[...]).astype(BF16)
    acc = h
    for f in range(0, D_FF, FF_CHUNK):
        gate = _dot(u, wg_ref[:, f:f + FF_CHUNK])
        up = _dot(u, wu_ref[:, f:f + FF_CHUNK])
        a = (_silu(gate) * up).astype(BF16)
        acc = acc + _dot(a, wd_ref[f:f + FF_CHUNK, :])
    pg = _sigmoid(_dot(_rms(acc, gp_ref[...]).astype(BF16), wpg_ref[...]))
    out = acc + _dot(p_ref[...].astype(BF16), wpp_ref[...]) * pg
    if final:
        out = _rms(out, gfin_ref[...])
    o_ref[...] = out


def _ffn_ple(h, p, gf, wg, wu, wd, gp, wpg, wpp, gfin, tm, final):
    m = h.shape[0]
    return pl.pallas_call(
        functools.partial(_ffn_ple_kernel, final=final),
        grid=(m // tm,),
        in_specs=[pl.BlockSpec((tm, D_MODEL), lambda i: (i, 0)),
                  pl.BlockSpec((tm, PLE_DIM), lambda i: (i, 0)),
                  _full((1, D_MODEL)), _full((D_MODEL, D_FF)), _full((D_MODEL, D_FF)),
                  _full((D_FF, D_MODEL)), _full((1, D_MODEL)), _full((D_MODEL, D_MODEL)),
                  _full((PLE_DIM, D_MODEL)), _full((1, D_MODEL))],
        out_specs=pl.BlockSpec((tm, D_MODEL), lambda i: (i, 0)),
        out_shape=jax.ShapeDtypeStruct((m, D_MODEL), F32),
        compiler_params=_cparams(1),
        name="ffn_ple",
    )(h, p, gf, wg, wu, wd, gp, wpg, wpp, gfin)


def _gated_norm(y, gate, ng):
    yg = y * gate
    parts = []
    for g in range(N_GROUPS):
        blk = yg[:, g * GROUP_W:(g + 1) * GROUP_W]
        ms = jnp.mean(blk * blk, axis=-1, keepdims=True)
        parts.append(blk * lax.rsqrt(ms + EPS))
    return (jnp.concatenate(parts, axis=-1) * ng).astype(BF16)


def _lane_block_ids(shape, width):
    return lax.broadcasted_iota(jnp.int32, shape, 1) // width


def _expand_heads(v, g, rows):
    blk = _lane_block_ids((rows, GROUP_W), HEAD_DIM)
    out = jnp.zeros((rows, GROUP_W), F32)
    for r in range(HEADS_PER_GROUP):
        hh = g * HEADS_PER_GROUP + r
        col = jnp.broadcast_to(v[:, hh:hh + 1], (rows, GROUP_W))
        out = jnp.where(blk == r, col, out)
    return out


def _mamba_prompt_kernel(hn_ref, hc_ref, g_ref, wzx_ref, wdt_ref, dtb_ref, alog_ref, wcv_ref,
                         bcv_ref, dsk_ref, ng_ref, wout_ref,
                         o_ref, st_ref, cs_ref,
                         z_scr, xb_scr, dt_scr, u_scr, gz_scr, dtc_scr, x_scr, b_scr, c_scr, y_scr,
                         s_scr, *, tl, nt):
    s = pl.program_id(0)
    q = SSD_CHUNK
    nk = SSM_CONV_WIDTH - 1
    first_in_seq = lax.rem(jnp.maximum(s - 1, 0), nt) == 0
    next_starts_seq = lax.rem(s, nt) == 0

    @pl.when(s == 0)
    def _():
        z_scr[...] = jnp.zeros(z_scr.shape, F32)
        xb_scr[...] = jnp.zeros(xb_scr.shape, F32)
        dt_scr[...] = jnp.zeros(dt_scr.shape, F32)
        s_scr[...] = jnp.zeros(s_scr.shape, F32)

    gz_scr[...] = _silu(z_scr[...])
    dtc_scr[...] = dt_scr[...]
    for k in range(nk):
        r = tl + SSM_EDGE - (nk - k) * SUBLANES + SUBLANES - 1
        cs_ref[k:k + 1, :] = xb_scr[r:r + 1, :]

    cw = 4 * LANES
    sub = lax.broadcasted_iota(jnp.int32, (SUBLANES, cw), 0)
    for cb0 in range(0, SSM_CONV_DIM, cw):
        cl = slice(cb0, cb0 + cw)
        for c in range(tl // q):
            base = c * q
            cur = xb_scr[base + SSM_EDGE:base + SSM_EDGE + q, cl]
            edges = []
            for m in range(nk, 0, -1):
                jm = PERM_VROWS - m
                cur_v = cur[jm * SUBLANES:(jm + 1) * SUBLANES, :]
                prv_v = xb_scr[base + (nk - m) * SUBLANES:base + (nk - m + 1) * SUBLANES, cl]
                edges.append(_edge_vrow(cur_v, prv_v, m, sub))
            hist = jnp.concatenate(edges + [cur], axis=0)
            conv = bcv_ref[:, cl] + cur * wcv_ref[nk:nk + 1, cl]
            for s in range(1, SSM_CONV_WIDTH):
                lo = SSM_EDGE - s * SUBLANES
                conv = conv + hist[lo:lo + q, :] * wcv_ref[nk - s:nk - s + 1, cl]
            act = _silu(conv)
            rows = slice(base, base + q)
            if cb0 < D_INNER:
                x_scr[rows, cl] = act
            elif cb0 < D_INNER + N_GROUPS * D_STATE:
                b_scr[rows, cb0 - D_INNER:cb0 - D_INNER + cw] = act.astype(BF16)
            else:
                c0 = cb0 - D_INNER - N_GROUPS * D_STATE
                c_scr[rows, c0:c0 + cw] = act.astype(BF16)

    u_scr[...] = _reorder_rows(_perm_matrix(), _rms(hn_ref[...], g_ref[...]).astype(BF16))
    dt_scr[...] = _softplus(_dot(u_scr[...], wdt_ref[...]) + dtb_ref[...])
    slab_w = 4 * LANES
    slabs = list(range(0, D_INNER + SSM_CONV_DIM, slab_w))

    def emit_in_proj_slab():
        if not slabs:
            return
        c0 = slabs.pop(0)
        res = _dot(u_scr[...], wzx_ref[:, c0:c0 + slab_w])
        if c0 < D_INNER:
            z_scr[:, c0:c0 + slab_w] = res
        else:
            xc = slice(c0 - D_INNER, c0 - D_INNER + slab_w)
            edge = jnp.where(next_starts_seq, 0.0, xb_scr[tl:tl + SSM_EDGE, xc])
            xb_scr[SSM_EDGE:SSM_EDGE + tl, xc] = res
            xb_scr[0:SSM_EDGE, xc] = edge

    a_neg = -jnp.exp(alog_ref[...])
    causal = _time_of_row((q, q), 0) >= _time_of_row((q, q), 1)
    tril = causal.astype(BF16)
    low_half = lax.broadcasted_iota(jnp.int32, (q, LANES), 1) < HEAD_DIM
    blk = _lane_block_ids((q, GROUP_W), HEAD_DIM)

    def pair_expand(cols):
        return jnp.concatenate([jnp.where(low_half, cols[0], cols[1]),
                                jnp.where(low_half, cols[2], cols[3])], axis=-1)

    for c in range(tl // q):
        rows = slice(c * q, (c + 1) * q)
        dtc = dtc_scr[rows, :]
        a = dtc * a_neg
        a_hi, a_mid, a_lo = _split3(a)
        a_cum = _dot(tril, a_hi) + _dot(tril, a_mid) + _dot(tril, a_lo)
        a_cum_t = a_cum.T
        for g in range(N_GROUPS):
            gs = slice(g * GROUP_W, (g + 1) * GROUP_W)
            ns = slice(g * D_STATE, (g + 1) * D_STATE)
            heads = [g * HEADS_PER_GROUP + r for r in range(HEADS_PER_GROUP)]
            bg = b_scr[rows, ns]
            cg = c_scr[rows, ns]
            xg = x_scr[rows, gs]
            sg = s_scr[gs, :]
            if c == 0:
                sg = jnp.where(first_in_seq, 0.0, sg)
            cb = _dot_nt(cg, bg)
            y_off = _dot_nt(cg, sg.astype(BF16))
            bc_a = [jnp.broadcast_to(a_cum[:, hh:hh + 1], (q, LANES)) for hh in heads]
            bc_dt = [jnp.broadcast_to(dtc[:, hh:hh + 1], (q, LANES)) for hh in heads]
            a_e = pair_expand(bc_a)
            xdt = xg * pair_expand(bc_dt)
            ms = []
            for r, hh in enumerate(heads):
                seg = bc_a[r] - a_cum_t[hh:hh + 1, :]
                lm = jnp.exp(jnp.where(causal, seg, -jnp.inf))
                ms.append((cb * lm).astype(BF16))
            xblk = jnp.concatenate(
                [jnp.where(blk == r, xdt, 0.0).astype(BF16) for r in range(HEADS_PER_GROUP)], axis=0)
            y = (_dot(jnp.concatenate(ms, axis=-1), xblk) + y_off * jnp.exp(a_e)
                 + xg * dsk_ref[:, gs])
            y_scr[rows, gs] = y
            xs = (xdt * jnp.exp(a_e[q - 1:q, :] - a_e)).astype(BF16)
            add = _dot_tn(xs, bg)
            for r, hh in enumerate(heads):
                cd = jnp.exp(bc_a[r][q - 1:q, :])
                hs = slice(hh * HEAD_DIM, (hh + 1) * HEAD_DIM)
                ps = slice(r * HEAD_DIM, (r + 1) * HEAD_DIM)
                s_scr[hs, :] = cd * sg[ps, :] + add[ps, :]
            emit_in_proj_slab()

    while slabs:
        emit_in_proj_slab()
    st_ref[...] = s_scr[...]
    yn = _reorder_rows(_unperm_matrix(), _gated_norm(y_scr[...], gz_scr[...], ng_ref[...]))
    o_ref[...] = hc_ref[...] + _dot(yn, wout_ref[...])


def _mamba_prompt(h, g, wzx, wdt, dtb, alog, wcv, bcv, dsk, ng, wout, tl):
    bsz, seq, _ = h.shape
    nt = seq // tl
    n_tiles = bsz * nt
    nconv = SSM_CONV_WIDTH - 1

    def cur(s):
        return jnp.maximum(s - 1, 0)

    def nxt(s):
        return jnp.minimum(s, n_tiles - 1)

    tile_cur = pl.BlockSpec((None, tl, D_MODEL), lambda s: (cur(s) // nt, cur(s) % nt, 0))
    tile_nxt = pl.BlockSpec((None, tl, D_MODEL), lambda s: (nxt(s) // nt, nxt(s) % nt, 0))
    return pl.pallas_call(
        functools.partial(_mamba_prompt_kernel, tl=tl, nt=nt),
        grid=(n_tiles + 1,),
        in_specs=[tile_nxt, tile_cur, _full((1, D_MODEL)),
                  _full((D_MODEL, D_INNER + SSM_CONV_DIM)),
                  _full((D_MODEL, LANES)), _full((1, LANES)), _full((1, LANES)),
                  _full((SSM_CONV_WIDTH, SSM_CONV_DIM)), _full((1, SSM_CONV_DIM)),
                  _full((1, D_INNER)), _full((1, D_INNER)), _full((D_INNER, D_MODEL))],
        out_specs=[tile_cur,
                   pl.BlockSpec((None, D_INNER, D_STATE), lambda s: (cur(s) // nt, 0, 0)),
                   pl.BlockSpec((None, nconv, SSM_CONV_DIM), lambda s: (cur(s) // nt, 0, 0))],
        out_shape=[jax.ShapeDtypeStruct(h.shape, F32),
                   jax.ShapeDtypeStruct((bsz, D_INNER, D_STATE), F32),
                   jax.ShapeDtypeStruct((bsz, nconv, SSM_CONV_DIM), F32)],
        scratch_shapes=[pltpu.VMEM((tl, D_INNER), F32),
                        pltpu.VMEM((tl + SSM_EDGE, SSM_CONV_DIM), F32),
                        pltpu.VMEM((tl, LANES), F32),
                        pltpu.VMEM((tl, D_MODEL), BF16),
                        pltpu.VMEM((tl, D_INNER), F32),
                        pltpu.VMEM((tl, LANES), F32),
                        pltpu.VMEM((tl, D_INNER), F32),
                        pltpu.VMEM((tl, N_GROUPS * D_STATE), BF16),
                        pltpu.VMEM((tl, N_GROUPS * D_STATE), BF16),
                        pltpu.VMEM((tl, D_INNER), F32),
                        pltpu.VMEM((D_INNER, D_STATE), F32)],
        compiler_params=_cparams(1),
        name="mamba_prompt",
    )(h, h, g, wzx, wdt, dtb, alog, wcv, bcv, dsk, ng, wout)


def _mamba_step_in_kernel(h_ref, g_ref, wzx_ref, wdt_ref, dtb_ref, alog_ref, wcv_ref, bcv_ref,
                          cs_ref, z_ref, act_ref, dt_ref, da_ref, ncs_ref):
    u = _rms(h_ref[...], g_ref[...]).astype(BF16)
    z_ref[...] = _dot(u, wzx_ref[:, :D_INNER])
    xb = _dot(u, wzx_ref[:, D_INNER:])
    dt = _softplus(_dot(u, wdt_ref[...]) + dtb_ref[...])
    dt_ref[...] = dt
    da_ref[...] = jnp.exp(dt * (-jnp.exp(alog_ref[...])))
    nk = SSM_CONV_WIDTH - 1
    cw = SSM_CONV_DIM
    conv = bcv_ref[...] + xb * wcv_ref[nk:nk + 1, :]
    for k in range(nk):
        conv = conv + cs_ref[:, k * cw:(k + 1) * cw] * wcv_ref[k:k + 1, :]
    act_ref[...] = _silu(conv)
    ncs_ref[:, 0:(nk - 1) * cw] = cs_ref[:, cw:nk * cw]
    ncs_ref[:, (nk - 1) * cw:nk * cw] = xb


def _mamba_step_in(h, g, wzx, wdt, dtb, alog, wcv, bcv, cs):
    bsz = h.shape[0]
    cw = (SSM_CONV_WIDTH - 1) * SSM_CONV_DIM
    shapes = [(bsz, D_INNER), (bsz, SSM_CONV_DIM), (bsz, LANES), (bsz, LANES), (bsz, cw)]
    return pl.pallas_call(
        _mamba_step_in_kernel,
        grid=(1,),
        in_specs=[_full((bsz, D_MODEL)), _full((1, D_MODEL)),
                  _full((D_MODEL, D_INNER + SSM_CONV_DIM)), _full((D_MODEL, LANES)),
                  _full((1, LANES)), _full((1, LANES)),
                  _full((SSM_CONV_WIDTH, SSM_CONV_DIM)), _full((1, SSM_CONV_DIM)),
                  _full((bsz, cw))],
        out_specs=[_full(s) for s in shapes],
        out_shape=[jax.ShapeDtypeStruct(s, F32) for s in shapes],
        compiler_params=_cparams(1),
        name="mamba_step_in",
    )(h, g, wzx, wdt, dtb, alog, wcv, bcv, cs)


def _ssm_step_kernel(da_ref, s_ref, act_ref, dt_ref, dsk_ref, *rest, tb, layer, fill):
    ns_ref, y_ref = rest[-2:]
    _own_slot_or_zero(
        fill, layer, ns_ref,
        lambda: _ssm_step_body(da_ref, s_ref, act_ref, dt_ref, dsk_ref, ns_ref, y_ref, tb))


def _ssm_step_body(da_ref, s_ref, act_ref, dt_ref, dsk_ref, ns_ref, y_ref, tb):
    i = pl.program_id(1)
    rid = lax.broadcasted_iota(jnp.int32, (tb, 1), 0)
    zpad = jnp.zeros((tb, GROUP_W), F32)
    dt = dt_ref[...]
    for g in range(N_GROUPS):
        gs = slice(g * GROUP_W, (g + 1) * GROUP_W)
        xg = act_ref[:, gs]
        bg = act_ref[:, D_INNER + g * D_STATE:D_INNER + (g + 1) * D_STATE]
        cg = act_ref[:, D_INNER + (N_GROUPS + g) * D_STATE:D_INNER + (N_GROUPS + g + 1) * D_STATE]
        xdt = xg * _expand_heads(dt, g, tb)
        y8 = jnp.zeros((tb, GROUP_W), F32)
        for r in range(tb):
            sel = rid == r
            x1 = jnp.concatenate([jnp.where(sel, xdt, 0.0), zpad], axis=0).astype(BF16)
            b1 = jnp.concatenate([jnp.where(sel, bg, 0.0), zpad[:, :D_STATE]], axis=0).astype(BF16)
            c1 = jnp.concatenate([jnp.where(sel, cg, 0.0), zpad[:, :D_STATE]], axis=0).astype(BF16)
            add = _dot_tn(x1, b1)
            news = []
            for k in range(HEADS_PER_GROUP):
                hh = g * HEADS_PER_GROUP + k
                da = da_ref[i * tb + r, hh]
                sn = da * s_ref[r, hh] + add[k * HEAD_DIM:(k + 1) * HEAD_DIM, :]
                ns_ref[r, hh] = sn
                news.append(sn)
            sg = jnp.concatenate(news, axis=0).astype(BF16)
            y8 = y8 + _dot_nt(c1, sg)[:tb, :]
        y_ref[:, gs] = y8 + xg * dsk_ref[:, gs]


def _ssm_step(da, states, layer, act, dt, dsk, tb, prev_out):
    n_layers, bsz = states.shape[:2]
    fill = prev_out is None
    grid, bi, li = _stack_plan(layer, n_layers, bsz // tb, fill)
    sblk = (None, tb, N_HEADS, HEAD_DIM, D_STATE)
    in_specs = [pl.BlockSpec(memory_space=pltpu.SMEM),
                pl.BlockSpec(sblk, lambda ls, i: (layer, bi(ls, i), 0, 0, 0)),
                pl.BlockSpec((tb, SSM_CONV_DIM), lambda ls, i: (bi(ls, i), 0)),
                pl.BlockSpec((tb, LANES), lambda ls, i: (bi(ls, i), 0)),
                _full((1, D_INNER))]
    args = [da, states, act, dt, dsk]
    aliases = {}
    if not fill:
        in_specs.append(pl.BlockSpec(memory_space=pl.ANY))
        args.append(prev_out)
        aliases = {len(args) - 1: 0}
    return pl.pallas_call(
        functools.partial(_ssm_step_kernel, tb=tb, layer=layer, fill=fill),
        grid=grid,
        in_specs=in_specs,
        out_specs=[pl.BlockSpec(sblk, lambda ls, i: (li(ls), i, 0, 0, 0)),
                   pl.BlockSpec((tb, D_INNER), lambda ls, i: (bi(ls, i), 0))],
        out_shape=[jax.ShapeDtypeStruct(states.shape, F32),
                   jax.ShapeDtypeStruct((bsz, D_INNER), F32)],
        input_output_aliases=aliases,
        compiler_params=_cparams(2),
        name="ssm_step",
    )(*args)


def _gnorm_out_kernel(y_ref, z_ref, h_ref, ng_ref, wout_ref, o_ref):
    yn = _gated_norm(y_ref[...], _silu(z_ref[...]), ng_ref[...])
    o_ref[...] = h_ref[...] + _dot(yn, wout_ref[...])


def _gnorm_out(y, z, h, ng, wout):
    bsz = y.shape[0]
    return pl.pallas_call(
        _gnorm_out_kernel,
        grid=(1,),
        in_specs=[_full((bsz, D_INNER)), _full((bsz, D_INNER)), _full((bsz, D_MODEL)),
                  _full((1, D_INNER)), _full((D_INNER, D_MODEL))],
        out_specs=_full((bsz, D_MODEL)),
        out_shape=jax.ShapeDtypeStruct((bsz, D_MODEL), F32),
        compiler_params=_cparams(1),
        name="gnorm_out",
    )(y, z, h, ng, wout)


def _row(x):
    return x.reshape(1, -1)


def _pad_lanes(x):
    return jnp.pad(x, [(0, 0)] * (x.ndim - 1) + [(0, LANES - x.shape[-1])])


def kernel(x_prompt, x_sample, p_prompt, p_sample, state_conv_mixer, state_ssm, state_ssm_conv, g_mix, g_ffn, g_ple, g_final, cv_w_pw1, cv_b_pw1, cv_w_dw, cv_b_dw, cv_ln_g, cv_ln_b, cv_w_pw2, cv_b_pw2, ssm_w_in, ssm_w_conv, ssm_b_conv, ssm_dt_bias, ssm_a_log, ssm_d, ssm_norm_g, ssm_w_out, ffn_w_gate, ffn_w_up, ffn_w_down, ple_w_proj, ple_w_gate):
    depth = g_mix.shape[0]
    bp, seq, _ = x_prompt.shape
    bs = x_sample.shape[0]
    mp = bp * seq
    tm_prompt = 512
    tl_conv = 512
    tl_ssm = 256
    zx_w = D_INNER + SSM_CONV_DIM

    hp = x_prompt
    hs = x_sample.reshape(bs, D_MODEL)
    conv_p, ssm_p, ssmc_p, ssmc_s = [], [], [], []
    conv_s = ssm_s = None
    gfin = _row(g_final)

    for i in range(depth):
        j = i // 2
        gm = _row(g_mix[i])
        if i % 2 == 0:
            w1 = cv_w_pw1[j].astype(BF16)
            b1 = _row(cv_b_pw1[j])
            w2 = cv_w_pw2[j].astype(BF16)
            cargs = (cv_w_dw[j], _row(cv_b_dw[j]), _row(cv_ln_g[j]), _row(cv_ln_b[j]),
                     w2, _row(cv_b_pw2[j]))
            vp = _pw1_glu(hp.reshape(mp, D_MODEL), gm, w1, b1, tm_prompt, True).reshape(bp, seq, D_MODEL)
            tail = vp[:, seq - PERM_ROWS:, :].reshape(bp, PERM_VROWS, SUBLANES, D_MODEL)
            tail = jnp.swapaxes(tail, 1, 2).reshape(bp, PERM_ROWS, D_MODEL)
            conv_p.append(tail[:, PERM_ROWS - (CONV_WIDTH - 1):, :])
            hp = _conv_prompt(vp, hp, *cargs, tl_conv)
            vs = _pw1_glu(hs, gm, w1, b1, bs, False)
            hs, conv_s = _conv_step(vs, hs, state_conv_mixer, j, *cargs, 16, conv_s)
        else:
            w_in = ssm_w_in[j]
            wzx = w_in[:, :zx_w].astype(BF16)
            wdt = _pad_lanes(w_in[:, zx_w:]).astype(BF16)
            dtb = _pad_lanes(_row(ssm_dt_bias[j]))
            alog = _pad_lanes(_row(ssm_a_log[j]))
            wcv = ssm_w_conv[j]
            bcv = _row(ssm_b_conv[j])
            dsk = _row(jnp.repeat(ssm_d[j], HEAD_DIM))
            ng = _row(ssm_norm_g[j])
            wout = ssm_w_out[j].astype(BF16)
            hp, fin, ncs = _mamba_prompt(hp, gm, wzx, wdt, dtb, alog, wcv, bcv, dsk, ng, wout, tl_ssm)
            ssm_p.append(fin.reshape(bp, N_HEADS, HEAD_DIM, D_STATE))
            ssmc_p.append(ncs)
            cs = state_ssm_conv[j].reshape(bs, (SSM_CONV_WIDTH - 1) * SSM_CONV_DIM)
            z, act, dt, da, ncs_s = _mamba_step_in(hs, gm, wzx, wdt, dtb, alog, wcv, bcv, cs)
            ssm_s, y = _ssm_step(da[:, :N_HEADS], state_ssm, j, act, dt, dsk, 8, ssm_s)
            hs = _gnorm_out(y, z, hs, ng, wout)
            ssmc_s.append(ncs_s.reshape(bs, SSM_CONV_WIDTH - 1, SSM_CONV_DIM))
        fargs = (_row(g_ffn[i]), ffn_w_gate[i].astype(BF16), ffn_w_up[i].astype(BF16),
                 ffn_w_down[i].astype(BF16), _row(g_ple[i]), ple_w_gate[i].astype(BF16),
                 ple_w_proj[i].astype(BF16), gfin)
        final = i == depth - 1
        hp = _ffn_ple(hp.reshape(mp, D_MODEL), p_prompt[i].reshape(mp, PLE_DIM), *fargs,
                      tm_prompt, final).reshape(bp, seq, D_MODEL)
        hs = _ffn_ple(hs, p_sample[i].reshape(bs, PLE_DIM), *fargs, bs, final)

    return (hp, hs.reshape(bs, 1, D_MODEL),
            jnp.stack(conv_p), jnp.stack(ssm_p), jnp.stack(ssmc_p),
            conv_s, ssm_s, jnp.stack(ssmc_s))
```

```python
import functools

import jax
import jax.numpy as jnp
from jax import lax
from jax.experimental import pallas as pl
from jax.experimental.pallas import tpu as pltpu

F32 = jnp.float32
BF16 = jnp.bfloat16

D_MODEL = 1024
CONV_WIDTH = 31
D_INNER = 2048
HEAD_DIM = 64
N_HEADS = 32
N_GROUPS = 8
HEADS_PER_GROUP = 4
D_STATE = 128
SSM_CONV_WIDTH = 4
SSM_CONV_DIM = 4096
D_FF = 2816
PLE_DIM = 256
EPS = 1e-6

LANES = 128
SUBLANES = 8
GROUP_W = HEADS_PER_GROUP * HEAD_DIM
FF_CHUNK = 256
SSD_CHUNK = 128
PERM_ROWS = SSD_CHUNK
PERM_VROWS = PERM_ROWS // SUBLANES
SSM_EDGE = (SSM_CONV_WIDTH - 1) * SUBLANES
VMEM_LIMIT = 60 * 1024 * 1024


def _cparams(n_axes):
    return pltpu.CompilerParams(
        dimension_semantics=("arbitrary",) * n_axes, vmem_limit_bytes=VMEM_LIMIT)


def _sigmoid(x):
    return 1.0 / (1.0 + jnp.exp(-x))


def _silu(x):
    return x * _sigmoid(x)


def _softplus(x):
    return jnp.maximum(x, 0.0) + jnp.log(1.0 + jnp.exp(-jnp.abs(x)))


def _rms(x, g):
    return x * lax.rsqrt(jnp.mean(x * x, axis=-1, keepdims=True) + EPS) * g


def _dot(a, b):
    return jnp.dot(a, b, preferred_element_type=F32)


def _dot_nt(a, b):
    return lax.dot_general(a, b, (((1,), (1,)), ((), ())), preferred_element_type=F32)


def _dot_tn(a, b):
    return lax.dot_general(a, b, (((0,), (0,)), ((), ())), preferred_element_type=F32)


def _split3(x):
    hi = x.astype(BF16)
    r1 = x - hi.astype(F32)
    mid = r1.astype(BF16)
    lo = (r1 - mid.astype(F32)).astype(BF16)
    return hi, mid, lo


def _full(shape):
    return pl.BlockSpec(shape, lambda *_: (0,) * len(shape))


def _layer(shape, layer):
    return pl.BlockSpec((None,) + tuple(shape), lambda *_: (layer,) + (0,) * len(shape))


def _time_of_row(shape, axis):
    r = lax.broadcasted_iota(jnp.int32, shape, axis)
    return PERM_VROWS * (r & (SUBLANES - 1)) + lax.shift_right_logical(r, 3)


def _perm_matrix():
    col = lax.broadcasted_iota(jnp.int32, (PERM_ROWS, PERM_ROWS), 1)
    return (col == _time_of_row((PERM_ROWS, PERM_ROWS), 0)).astype(BF16)


def _unperm_matrix():
    row = lax.broadcasted_iota(jnp.int32, (PERM_ROWS, PERM_ROWS), 0)
    return (row == _time_of_row((PERM_ROWS, PERM_ROWS), 1)).astype(BF16)


def _reorder_rows(mat, x):
    blocks = [_dot(mat, x[c:c + PERM_ROWS, :]).astype(BF16) for c in range(0, x.shape[0], PERM_ROWS)]
    return blocks[0] if len(blocks) == 1 else jnp.concatenate(blocks, axis=0)


def _edge_vrow(cur, prv, m, sub):
    s = 1 if m <= PERM_VROWS else 2
    return jnp.where(sub < s, pltpu.roll(prv, s, 0), pltpu.roll(cur, s, 0))


def _pw1_glu_kernel(h_ref, g_ref, w_ref, b_ref, v_ref, *, permute):
    u = _rms(h_ref[...], g_ref[...]).astype(BF16)
    if permute:
        u = _reorder_rows(_perm_matrix(), u)
    a = _dot(u, w_ref[...]) + b_ref[...]
    v_ref[...] = a[:, :D_MODEL] * _sigmoid(a[:, D_MODEL:])


def _pw1_glu(h, g, w, b, mix_layer, conv_layer, tm, permute):
    m = h.shape[0]
    return pl.pallas_call(
        functools.partial(_pw1_glu_kernel, permute=permute),
        grid=(m // tm,),
        in_specs=[pl.BlockSpec((tm, D_MODEL), lambda i: (i, 0)),
                  _layer((1, D_MODEL), mix_layer), _layer((D_MODEL, 2 * D_MODEL), conv_layer),
                  _layer((1, 2 * D_MODEL), conv_layer)],
        out_specs=pl.BlockSpec((tm, D_MODEL), lambda i: (i, 0)),
        out_shape=jax.ShapeDtypeStruct((m, D_MODEL), F32),
        compiler_params=_cparams(1),
        name="pw1_glu",
    )(h, g, w, b)


def _conv_weight_specs(layer):
    vec = _layer((1, D_MODEL), layer)
    return [_layer((CONV_WIDTH, D_MODEL), layer), vec, vec, vec,
            _layer((D_MODEL, D_MODEL), layer), vec]


def _ln_silu(c, lng, lnb):
    mu = jnp.mean(c, axis=-1, keepdims=True)
    d = c - mu
    var = jnp.mean(d * d, axis=-1, keepdims=True)
    return _silu(d * lax.rsqrt(var + EPS) * lng + lnb)


def _conv_prompt_kernel(v_ref, h_ref, wdw_ref, bdw_ref, lng_ref, lnb_ref, w2_ref, b2_ref,
                        o_ref, xs, ext, wb, cbuf, *, tl):
    l = pl.program_id(1)
    nb = CONV_WIDTH - 1

    @pl.when((pl.program_id(0) == 0) & (l == 0))
    def _():
        for k in range(CONV_WIDTH):
            wb[k * SUBLANES:(k + 1) * SUBLANES, :] = jnp.broadcast_to(
                wdw_ref[k:k + 1, :], (SUBLANES, D_MODEL))

    @pl.when(l == 0)
    def _():
        xs[0:PERM_ROWS, :] = jnp.zeros((PERM_ROWS, D_MODEL), F32)

    @pl.when(l > 0)
    def _():
        xs[0:PERM_ROWS, :] = xs[tl:tl + PERM_ROWS, :]

    xs[PERM_ROWS:PERM_ROWS + tl, :] = v_ref[...]
    sub = lax.broadcasted_iota(jnp.int32, (SUBLANES, D_MODEL), 0)
    lw = 2 * LANES

    def block(c, carry):
        r0 = pl.multiple_of(c * PERM_ROWS, PERM_ROWS)
        for m in range(1, nb + 1):
            jm = (PERM_VROWS if m <= PERM_VROWS else 2 * PERM_VROWS) - m
            cur = xs[pl.ds(r0 + PERM_ROWS + SUBLANES * jm, SUBLANES), :]
            prv = xs[pl.ds(r0 + SUBLANES * jm, SUBLANES), :]
            ext[(nb - m) * SUBLANES:(nb - m + 1) * SUBLANES, :] = _edge_vrow(cur, prv, m, sub)
        ext[nb * SUBLANES:nb * SUBLANES + PERM_ROWS, :] = xs[pl.ds(r0 + PERM_ROWS, PERM_ROWS), :]
        for lb in range(0, D_MODEL, lw):
            accs = [jnp.broadcast_to(bdw_ref[:, lb:lb + lw], (SUBLANES, lw))] * PERM_VROWS
            for k in range(CONV_WIDTH):
                w8 = wb[k * SUBLANES:(k + 1) * SUBLANES, lb:lb + lw]
                accs = [a + ext[(j + k) * SUBLANES:(j + k + 1) * SUBLANES, lb:lb + lw] * w8
                        for j, a in enumerate(accs)]
            for j, a in enumerate(accs):
                cbuf[pl.ds(r0 + j * SUBLANES, SUBLANES), lb:lb + lw] = a
        return carry

    lax.fori_loop(0, tl // PERM_ROWS, block, 0)
    y = _ln_silu(cbuf[...], lng_ref[...], lnb_ref[...]).astype(BF16)
    y = _reorder_rows(_unperm_matrix(), y)
    o_ref[...] = h_ref[...] + _dot(y, w2_ref[...]) + b2_ref[...]


def _conv_prompt(v, h, wdw, bdw, lng, lnb, w2, b2, layer, tl):
    bsz, seq, _ = v.shape
    tile = pl.BlockSpec((None, tl, D_MODEL), lambda b, l: (b, l, 0))
    return pl.pallas_call(
        functools.partial(_conv_prompt_kernel, tl=tl),
        grid=(bsz, seq // tl),
        in_specs=[tile, tile] + _conv_weight_specs(layer),
        out_specs=tile,
        out_shape=jax.ShapeDtypeStruct(v.shape, F32),
        scratch_shapes=[pltpu.VMEM((tl + PERM_ROWS, D_MODEL), F32),
                        pltpu.VMEM(((CONV_WIDTH - 1) * SUBLANES + PERM_ROWS, D_MODEL), F32),
                        pltpu.VMEM((CONV_WIDTH * SUBLANES, D_MODEL), F32),
                        pltpu.VMEM((tl, D_MODEL), F32)],
        compiler_params=_cparams(2),
        name="conv_prompt",
    )(v, h, wdw, bdw, lng, lnb, w2, b2)


def _stack_plan(layer, n_layers, nb, fill):
    park = 0 if layer > 0 else nb - 1
    if fill:
        return (n_layers, nb), (lambda ls, i: jnp.where(ls == layer, i, park)), (lambda ls: ls)
    return (1, nb), (lambda ls, i: i), (lambda ls: layer)


def _own_slot_or_zero(fill, layer, state_out_ref, body):
    if not fill:
        body()
        return

    @pl.when(pl.program_id(0) != layer)
    def _():
        state_out_ref[...] = jnp.zeros(state_out_ref.shape, F32)

    pl.when(pl.program_id(0) == layer)(body)


def _conv_step_kernel(v_ref, h_ref, st_ref, wdw_ref, bdw_ref, lng_ref, lnb_ref, w2_ref, b2_ref,
                      *rest, layer, fill):
    o_ref, nst_ref = rest[-2:]
    nk = CONV_WIDTH - 1

    def body():
        v = v_ref[...]
        acc = bdw_ref[...] + v * wdw_ref[nk:nk + 1, :]
        for k in range(nk):
            acc = acc + st_ref[k] * wdw_ref[k:k + 1, :]
        nst_ref[0:nk - 1] = st_ref[1:nk]
        nst_ref[nk - 1] = v
        y = _ln_silu(acc, lng_ref[...], lnb_ref[...]).astype(BF16)
        o_ref[...] = h_ref[...] + _dot(y, w2_ref[...]) + b2_ref[...]

    _own_slot_or_zero(fill, layer, nst_ref, body)


def _conv_step(v, h, states, wdw, bdw, lng, lnb, w2, b2, layer, tb, prev_out):
    n_layers, _, bsz, _ = states.shape
    fill = prev_out is None
    grid, bi, li = _stack_plan(layer, n_layers, bsz // tb, fill)
    row = pl.BlockSpec((tb, D_MODEL), lambda ls, i: (bi(ls, i), 0))
    sblk = (None, CONV_WIDTH - 1, tb, D_MODEL)
    in_specs = [row, row, pl.BlockSpec(sblk, lambda ls, i: (layer, 0, bi(ls, i), 0))]
    in_specs += _conv_weight_specs(layer)
    args = [v, h, states, wdw, bdw, lng, lnb, w2, b2]
    aliases = {}
    if not fill:
        in_specs.append(pl.BlockSpec(memory_space=pl.ANY))
        args.append(prev_out)
        aliases = {len(args) - 1: 1}
    return pl.pallas_call(
        functools.partial(_conv_step_kernel, layer=layer, fill=fill),
        grid=grid,
        in_specs=in_specs,
        out_specs=[row, pl.BlockSpec(sblk, lambda ls, i: (li(ls), 0, i, 0))],
        out_shape=[jax.ShapeDtypeStruct((bsz, D_MODEL), F32),
                   jax.ShapeDtypeStruct(states.shape, F32)],
        input_output_aliases=aliases,
        compiler_params=_cparams(2),
        name="conv_step",
    )(*args)


def _ffn_ple_kernel(h_ref, p_ref, gf_ref, wg_ref, wu_ref, wd_ref, gp_ref, wpg_ref, wpp_ref,
                    gfin_ref, o_ref, *, final):
    h = h_ref[...]
    u = _rms(h, gf_ref[...]).astype(BF16)
    acc = h
    for f in range(0, D_FF, FF_CHUNK):
        gate = _dot(u, wg_ref[:, f:f + FF_CHUNK])
        up = _dot(u, wu_ref[:, f:f + FF_CHUNK])
        a = (_silu(gate) * up).astype(BF16)
        acc = acc + _dot(a, wd_ref[f:f + FF_CHUNK, :])
    pg = _sigmoid(_dot(_rms(acc, gp_ref[...]).astype(BF16), wpg_ref[...]))
    out = acc + _dot(p_ref[...].astype(BF16), wpp_ref[...]) * pg
    if final:
        out = _rms(out, gfin_ref[...])
    o_ref[...] = out


def _ffn_ple(h, p, gf, wg, wu, wd, gp, wpg, wpp, gfin, layer, tm, final):
    m = h.shape[0]
    vec = _layer((1, D_MODEL), layer)
    return pl.pallas_call(
        functools.partial(_ffn_ple_kernel, final=final),
        grid=(m // tm,),
        in_specs=[pl.BlockSpec((tm, D_MODEL), lambda i: (i, 0)),
                  pl.BlockSpec((None, tm, PLE_DIM), lambda i: (layer, i, 0)),
                  vec, _layer((D_MODEL, D_FF), layer), _layer((D_MODEL, D_FF), layer),
                  _layer((D_FF, D_MODEL), layer), vec, _layer((D_MODEL, D_MODEL), layer),
                  _layer((PLE_DIM, D_MODEL), layer), _full((1, D_MODEL))],
        out_specs=pl.BlockSpec((tm, D_MODEL), lambda i: (i, 0)),
        out_shape=jax.ShapeDtypeStruct((m, D_MODEL), F32),
        compiler_params=_cparams(1),
        name="ffn_ple",
    )(h, p, gf, wg, wu, wd, gp, wpg, wpp, gfin)


def _gated_norm(y, gate, ng):
    yg = y * gate
    parts = []
    for g in range(N_GROUPS):
        blk = yg[:, g * GROUP_W:(g + 1) * GROUP_W]
        ms = jnp.mean(blk * blk, axis=-1, keepdims=True)
        parts.append(blk * lax.rsqrt(ms + EPS))
    return (jnp.concatenate(parts, axis=-1) * ng).astype(BF16)


def _lane_block_ids(shape, width):
    return lax.broadcasted_iota(jnp.int32, shape, 1) // width


def _expand_heads(v, g, rows):
    blk = _lane_block_ids((rows, GROUP_W), HEAD_DIM)
    out = jnp.zeros((rows, GROUP_W), F32)
    for r in range(HEADS_PER_GROUP):
        hh = g * HEADS_PER_GROUP + r
        col = jnp.broadcast_to(v[:, hh:hh + 1], (rows, GROUP_W))
        out = jnp.where(blk == r, col, out)
    return out


def _mamba_prompt_kernel(hn_ref, hc_ref, g_ref, wzx_ref, wdt_ref, dtb_ref, alog_ref, wcv_ref,
                         bcv_ref, dsk_ref, ng_ref, wout_ref,
                         o_ref, st_ref, cs_ref,
                         z_scr, xb_scr, dt_scr, u_scr, gz_scr, dtc_scr, x_scr, b_scr, c_scr, y_scr,
                         s_scr, *, tl, nt):
    s = pl.program_id(0)
    q = SSD_CHUNK
    nk = SSM_CONV_WIDTH - 1
    first_in_seq = lax.rem(jnp.maximum(s - 1, 0), nt) == 0
    next_starts_seq = lax.rem(s, nt) == 0

    @pl.when(s == 0)
    def _():
        z_scr[...] = jnp.zeros(z_scr.shape, F32)
        xb_scr[...] = jnp.zeros(xb_scr.shape, F32)
        dt_scr[...] = jnp.zeros(dt_scr.shape, F32)
        s_scr[...] = jnp.zeros(s_scr.shape, F32)

    gz_scr[...] = _silu(z_scr[...])
    dtc_scr[...] = dt_scr[...]
    for k in range(nk):
        r = tl + SSM_EDGE - (nk - k) * SUBLANES + SUBLANES - 1
        cs_ref[k:k + 1, :] = xb_scr[r:r + 1, :]

    cw = 4 * LANES
    sub = lax.broadcasted_iota(jnp.int32, (SUBLANES, cw), 0)
    for cb0 in range(0, SSM_CONV_DIM, cw):
        cl = slice(cb0, cb0 + cw)
        for c in range(tl // q):
            base = c * q
            cur = xb_scr[base + SSM_EDGE:base + SSM_EDGE + q, cl]
            edges = []
            for m in range(nk, 0, -1):
                jm = PERM_VROWS - m
                cur_v = cur[jm * SUBLANES:(jm + 1) * SUBLANES, :]
                prv_v = xb_scr[base + (nk - m) * SUBLANES:base + (nk - m + 1) * SUBLANES, cl]
                edges.append(_edge_vrow(cur_v, prv_v, m, sub))
            hist = jnp.concatenate(edges + [cur], axis=0)
            conv = bcv_ref[:, cl] + cur * wcv_ref[nk:nk + 1, cl]
            for s in range(1, SSM_CONV_WIDTH):
                lo = SSM_EDGE - s * SUBLANES
                conv = conv + hist[lo:lo + q, :] * wcv_ref[nk - s:nk - s + 1, cl]
            act = _silu(conv)
            rows = slice(base, base + q)
            if cb0 < D_INNER:
                x_scr[rows, cl] = act
            elif cb0 < D_INNER + N_GROUPS * D_STATE:
                b_scr[rows, cb0 - D_INNER:cb0 - D_INNER + cw] = act.astype(BF16)
            else:
                c0 = cb0 - D_INNER - N_GROUPS * D_STATE
                c_scr[rows, c0:c0 + cw] = act.astype(BF16)

    u_scr[...] = _reorder_rows(_perm_matrix(), _rms(hn_ref[...], g_ref[...]).astype(BF16))
    dt_scr[...] = _softplus(_dot(u_scr[...], wdt_ref[...]) + dtb_ref[...])
    slab_w = 4 * LANES
    slabs = list(range(0, D_INNER + SSM_CONV_DIM, slab_w))

    def emit_in_proj_slab():
        if not slabs:
            return
        c0 = slabs.pop(0)
        res = _dot(u_scr[...], wzx_ref[:, c0:c0 + slab_w])
        if c0 < D_INNER:
            z_scr[:, c0:c0 + slab_w] = res
        else:
            xc = slice(c0 - D_INNER, c0 - D_INNER + slab_w)
            edge = jnp.where(next_starts_seq, 0.0, xb_scr[tl:tl + SSM_EDGE, xc])
            xb_scr[SSM_EDGE:SSM_EDGE + tl, xc] = res
            xb_scr[0:SSM_EDGE, xc] = edge

    a_neg = -jnp.exp(alog_ref[...])
    causal = _time_of_row((q, q), 0) >= _time_of_row((q, q), 1)
    tril = causal.astype(BF16)
    low_half = lax.broadcasted_iota(jnp.int32, (q, LANES), 1) < HEAD_DIM
    blk = _lane_block_ids((q, GROUP_W), HEAD_DIM)

    def pair_expand(cols):
        return jnp.concatenate([jnp.where(low_half, cols[0], cols[1]),
                                jnp.where(low_half, cols[2], cols[3])], axis=-1)

    for c in range(tl // q):
        rows = slice(c * q, (c + 1) * q)
        dtc = dtc_scr[rows, :]
        a = dtc * a_neg
        a_hi, a_mid, a_lo = _split3(a)
        a_cum = _dot(tril, a_hi) + _dot(tril, a_mid) + _dot(tril, a_lo)
        a_cum_t = a_cum.T
        for g in range(N_GROUPS):
            gs = slice(g * GROUP_W, (g + 1) * GROUP_W)
            ns = slice(g * D_STATE, (g + 1) * D_STATE)
            heads = [g * HEADS_PER_GROUP + r for r in range(HEADS_PER_GROUP)]
            bg = b_scr[rows, ns]
            cg = c_scr[rows, ns]
            xg = x_scr[rows, gs]
            sg = s_scr[gs, :]
            if c == 0:
                sg = jnp.where(first_in_seq, 0.0, sg)
            cb = _dot_nt(cg, bg)
            y_off = _dot_nt(cg, sg.astype(BF16))
            bc_a = [jnp.broadcast_to(a_cum[:, hh:hh + 1], (q, LANES)) for hh in heads]
            bc_dt = [jnp.broadcast_to(dtc[:, hh:hh + 1], (q, LANES)) for hh in heads]
            a_e = pair_expand(bc_a)
            xdt = xg * pair_expand(bc_dt)
            ms = []
            for r, hh in enumerate(heads):
                seg = bc_a[r] - a_cum_t[hh:hh + 1, :]
                lm = jnp.exp(jnp.where(causal, seg, -jnp.inf))
                ms.append((cb * lm).astype(BF16))
            xblk = jnp.concatenate(
                [jnp.where(blk == r, xdt, 0.0).astype(BF16) for r in range(HEADS_PER_GROUP)], axis=0)
            y = (_dot(jnp.concatenate(ms, axis=-1), xblk) + y_off * jnp.exp(a_e)
                 + xg * dsk_ref[:, gs])
            y_scr[rows, gs] = y
            xs = (xdt * jnp.exp(a_e[q - 1:q, :] - a_e)).astype(BF16)
            add = _dot_tn(xs, bg)
            for r, hh in enumerate(heads):
                cd = jnp.exp(bc_a[r][q - 1:q, :])
                hs = slice(hh * HEAD_DIM, (hh + 1) * HEAD_DIM)
                ps = slice(r * HEAD_DIM, (r + 1) * HEAD_DIM)
                s_scr[hs, :] = cd * sg[ps, :] + add[ps, :]
            emit_in_proj_slab()

    while slabs:
        emit_in_proj_slab()
    st_ref[...] = s_scr[...]
    yn = _reorder_rows(_unperm_matrix(), _gated_norm(y_scr[...], gz_scr[...], ng_ref[...]))
    o_ref[...] = hc_ref[...] + _dot(yn, wout_ref[...])


def _mamba_weight_specs(mix_layer, layer):
    return [_layer((1, D_MODEL), mix_layer),
            pl.BlockSpec((None, D_MODEL, D_INNER + SSM_CONV_DIM), lambda *_: (layer, 0, 0)),
            _layer((D_MODEL, LANES), layer), _layer((1, LANES), layer), _layer((1, LANES), layer),
            _layer((SSM_CONV_WIDTH, SSM_CONV_DIM), layer), _layer((1, SSM_CONV_DIM), layer)]


def _mamba_prompt(h, g, wzx, wdt, dtb, alog, wcv, bcv, dsk, ng, wout, mix_layer, layer, tl):
    bsz, seq, _ = h.shape
    nt = seq // tl
    n_tiles = bsz * nt
    nconv = SSM_CONV_WIDTH - 1

    def cur(s):
        return jnp.maximum(s - 1, 0)

    def nxt(s):
        return jnp.minimum(s, n_tiles - 1)

    tile_cur = pl.BlockSpec((None, tl, D_MODEL), lambda s: (cur(s) // nt, cur(s) % nt, 0))
    tile_nxt = pl.BlockSpec((None, tl, D_MODEL), lambda s: (nxt(s) // nt, nxt(s) % nt, 0))
    return pl.pallas_call(
        functools.partial(_mamba_prompt_kernel, tl=tl, nt=nt),
        grid=(n_tiles + 1,),
        in_specs=[tile_nxt, tile_cur] + _mamba_weight_specs(mix_layer, layer) + [
            _layer((1, D_INNER), layer), _layer((1, D_INNER), layer),
            _layer((D_INNER, D_MODEL), layer)],
        out_specs=[tile_cur,
                   pl.BlockSpec((None, D_INNER, D_STATE), lambda s: (cur(s) // nt, 0, 0)),
                   pl.BlockSpec((None, nconv, SSM_CONV_DIM), lambda s: (cur(s) // nt, 0, 0))],
        out_shape=[jax.ShapeDtypeStruct(h.shape, F32),
                   jax.ShapeDtypeStruct((bsz, D_INNER, D_STATE), F32),
                   jax.ShapeDtypeStruct((bsz, nconv, SSM_CONV_DIM), F32)],
        scratch_shapes=[pltpu.VMEM((tl, D_INNER), F32),
                        pltpu.VMEM((tl + SSM_EDGE, SSM_CONV_DIM), F32),
                        pltpu.VMEM((tl, LANES), F32),
                        pltpu.VMEM((tl, D_MODEL), BF16),
                        pltpu.VMEM((tl, D_INNER), F32),
                        pltpu.VMEM((tl, LANES), F32),
                        pltpu.VMEM((tl, D_INNER), F32),
                        pltpu.VMEM((tl, N_GROUPS * D_STATE), BF16),
                        pltpu.VMEM((tl, N_GROUPS * D_STATE), BF16),
                        pltpu.VMEM((tl, D_INNER), F32),
                        pltpu.VMEM((D_INNER, D_STATE), F32)],
        compiler_params=_cparams(1),
        name="mamba_prompt",
    )(h, h, g, wzx, wdt, dtb, alog, wcv, bcv, dsk, ng, wout)


def _mamba_step_in_kernel(h_ref, g_ref, wzx_ref, wdt_ref, dtb_ref, alog_ref, wcv_ref, bcv_ref,
                          cs_ref, *rest, layer, fill):
    z_ref, act_ref, dt_ref, da_ref, ncs_ref = rest[-5:]
    nk = SSM_CONV_WIDTH - 1

    def body():
        u = _rms(h_ref[...], g_ref[...]).astype(BF16)
        z_ref[...] = _dot(u, wzx_ref[:, :D_INNER])
        xb = _dot(u, wzx_ref[:, D_INNER:])
        dt = _softplus(_dot(u, wdt_ref[...]) + dtb_ref[...])
        dt_ref[...] = dt
        da_ref[...] = jnp.exp(dt * (-jnp.exp(alog_ref[...])))
        conv = bcv_ref[...] + xb * wcv_ref[nk:nk + 1, :]
        for k in range(nk):
            conv = conv + cs_ref[k] * wcv_ref[k:k + 1, :]
        act_ref[...] = _silu(conv)
        ncs_ref[0:nk - 1] = cs_ref[1:nk]
        ncs_ref[nk - 1] = xb

    _own_slot_or_zero(fill, layer, ncs_ref, body)


def _mamba_step_in(h, g, wzx, wdt, dtb, alog, wcv, bcv, states, mix_layer, layer, prev_out):
    n_layers, nk, bsz, _ = states.shape
    fill = prev_out is None
    grid, _, li = _stack_plan(layer, n_layers, 1, fill)
    sblk = (None, nk, bsz, SSM_CONV_DIM)
    shapes = [(bsz, D_INNER), (bsz, SSM_CONV_DIM), (bsz, LANES), (bsz, LANES)]
    in_specs = [_full((bsz, D_MODEL))] + _mamba_weight_specs(mix_layer, layer)
    in_specs.append(pl.BlockSpec(sblk, lambda ls, i: (layer, 0, 0, 0)))
    args = [h, g, wzx, wdt, dtb, alog, wcv, bcv, states]
    aliases = {}
    if not fill:
        in_specs.append(pl.BlockSpec(memory_space=pl.ANY))
        args.append(prev_out)
        aliases = {len(args) - 1: len(shapes)}
    return pl.pallas_call(
        functools.partial(_mamba_step_in_kernel, layer=layer, fill=fill),
        grid=grid,
        in_specs=in_specs,
        out_specs=[_full(s) for s in shapes] + [pl.BlockSpec(sblk, lambda ls, i: (li(ls), 0, 0, 0))],
        out_shape=[jax.ShapeDtypeStruct(s, F32) for s in shapes]
        + [jax.ShapeDtypeStruct(states.shape, F32)],
        input_output_aliases=aliases,
        compiler_params=_cparams(2),
        name="mamba_step_in",
    )(*args)


def _ssm_step_kernel(da_ref, s_ref, act_ref, dt_ref, dsk_ref, *rest, tb, layer, fill):
    ns_ref, y_ref = rest[-2:]
    _own_slot_or_zero(
        fill, layer, ns_ref,
        lambda: _ssm_step_body(da_ref, s_ref, act_ref, dt_ref, dsk_ref, ns_ref, y_ref, tb))


def _ssm_step_body(da_ref, s_ref, act_ref, dt_ref, dsk_ref, ns_ref, y_ref, tb):
    i = pl.program_id(1)
    rid = lax.broadcasted_iota(jnp.int32, (tb, 1), 0)
    zpad = jnp.zeros((tb, GROUP_W), F32)
    dt = dt_ref[...]
    for g in range(N_GROUPS):
        gs = slice(g * GROUP_W, (g + 1) * GROUP_W)
        xg = act_ref[:, gs]
        bg = act_ref[:, D_INNER + g * D_STATE:D_INNER + (g + 1) * D_STATE]
        cg = act_ref[:, D_INNER + (N_GROUPS + g) * D_STATE:D_INNER + (N_GROUPS + g + 1) * D_STATE]
        xdt = xg * _expand_heads(dt, g, tb)
        y8 = jnp.zeros((tb, GROUP_W), F32)
        for r in range(tb):
            sel = rid == r
            x1 = jnp.concatenate([jnp.where(sel, xdt, 0.0), zpad], axis=0).astype(BF16)
            b1 = jnp.concatenate([jnp.where(sel, bg, 0.0), zpad[:, :D_STATE]], axis=0).astype(BF16)
            c1 = jnp.concatenate([jnp.where(sel, cg, 0.0), zpad[:, :D_STATE]], axis=0).astype(BF16)
            add = _dot_tn(x1, b1)
            news = []
            for k in range(HEADS_PER_GROUP):
                hh = g * HEADS_PER_GROUP + k
                da = da_ref[i * tb + r, hh]
                sn = da * s_ref[r, hh] + add[k * HEAD_DIM:(k + 1) * HEAD_DIM, :]
                ns_ref[r, hh] = sn
                news.append(sn)
            sg = jnp.concatenate(news, axis=0).astype(BF16)
            y8 = y8 + _dot_nt(c1, sg)[:tb, :]
        y_ref[:, gs] = y8 + xg * dsk_ref[:, gs]


def _ssm_step(da, states, layer, act, dt, dsk, tb, prev_out):
    n_layers, bsz = states.shape[:2]
    fill = prev_out is None
    grid, bi, li = _stack_plan(layer, n_layers, bsz // tb, fill)
    sblk = (None, tb, N_HEADS, HEAD_DIM, D_STATE)
    in_specs = [pl.BlockSpec(memory_space=pltpu.SMEM),
                pl.BlockSpec(sblk, lambda ls, i: (layer, bi(ls, i), 0, 0, 0)),
                pl.BlockSpec((tb, SSM_CONV_DIM), lambda ls, i: (bi(ls, i), 0)),
                pl.BlockSpec((tb, LANES), lambda ls, i: (bi(ls, i), 0)),
                _layer((1, D_INNER), layer)]
    args = [da, states, act, dt, dsk]
    aliases = {}
    if not fill:
        in_specs.append(pl.BlockSpec(memory_space=pl.ANY))
        args.append(prev_out)
        aliases = {len(args) - 1: 0}
    return pl.pallas_call(
        functools.partial(_ssm_step_kernel, tb=tb, layer=layer, fill=fill),
        grid=grid,
        in_specs=in_specs,
        out_specs=[pl.BlockSpec(sblk, lambda ls, i: (li(ls), i, 0, 0, 0)),
                   pl.BlockSpec((tb, D_INNER), lambda ls, i: (bi(ls, i), 0))],
        out_shape=[jax.ShapeDtypeStruct(states.shape, F32),
                   jax.ShapeDtypeStruct((bsz, D_INNER), F32)],
        input_output_aliases=aliases,
        compiler_params=_cparams(2),
        name="ssm_step",
    )(*args)


def _gnorm_out_kernel(y_ref, z_ref, h_ref, ng_ref, wout_ref, o_ref):
    yn = _gated_norm(y_ref[...], _silu(z_ref[...]), ng_ref[...])
    o_ref[...] = h_ref[...] + _dot(yn, wout_ref[...])


def _gnorm_out(y, z, h, ng, wout, layer):
    bsz = y.shape[0]
    return pl.pallas_call(
        _gnorm_out_kernel,
        grid=(1,),
        in_specs=[_full((bsz, D_INNER)), _full((bsz, D_INNER)), _full((bsz, D_MODEL)),
                  _layer((1, D_INNER), layer), _layer((D_INNER, D_MODEL), layer)],
        out_specs=_full((bsz, D_MODEL)),
        out_shape=jax.ShapeDtypeStruct((bsz, D_MODEL), F32),
        compiler_params=_cparams(1),
        name="gnorm_out",
    )(y, z, h, ng, wout)


def _rows(x):
    return x.reshape(x.shape[0], 1, x.shape[1])


def _pad_lanes(x):
    return jnp.pad(x, [(0, 0)] * (x.ndim - 1) + [(0, LANES - x.shape[-1])])


def kernel(x_prompt, x_sample, p_prompt, p_sample, state_conv_mixer, state_ssm, state_ssm_conv, g_mix, g_ffn, g_ple, g_final, cv_w_pw1, cv_b_pw1, cv_w_dw, cv_b_dw, cv_ln_g, cv_ln_b, cv_w_pw2, cv_b_pw2, ssm_w_in, ssm_w_conv, ssm_b_conv, ssm_dt_bias, ssm_a_log, ssm_d, ssm_norm_g, ssm_w_out, ffn_w_gate, ffn_w_up, ffn_w_down, ple_w_proj, ple_w_gate):
    depth = g_mix.shape[0]
    bp, seq, _ = x_prompt.shape
    bs = x_sample.shape[0]
    mp = bp * seq
    tm_prompt = 512
    tl_conv = 512
    tl_ssm = 256
    zx_w = D_INNER + SSM_CONV_DIM

    gm, gf, gp = _rows(g_mix), _rows(g_ffn), _rows(g_ple)
    gfin = g_final.reshape(1, D_MODEL)
    w1, b1 = cv_w_pw1.astype(BF16), _rows(cv_b_pw1)
    cargs = (cv_w_dw, _rows(cv_b_dw), _rows(cv_ln_g), _rows(cv_ln_b),
             cv_w_pw2.astype(BF16), _rows(cv_b_pw2))
    w_in = ssm_w_in.astype(BF16)
    margs = (w_in, _pad_lanes(ssm_w_in[:, :, zx_w:]).astype(BF16), _rows(_pad_lanes(ssm_dt_bias)),
             _rows(_pad_lanes(ssm_a_log)), ssm_w_conv, _rows(ssm_b_conv))
    dsk = _rows(jnp.repeat(ssm_d, HEAD_DIM, axis=1))
    ng = _rows(ssm_norm_g)
    wout = ssm_w_out.astype(BF16)
    fargs = (gf, ffn_w_gate.astype(BF16), ffn_w_up.astype(BF16), ffn_w_down.astype(BF16), gp,
             ple_w_gate.astype(BF16), ple_w_proj.astype(BF16), gfin)
    pp = p_prompt.reshape(depth, mp, PLE_DIM)
    ps = p_sample.reshape(depth, bs, PLE_DIM)
    conv_st = jnp.swapaxes(state_conv_mixer, 1, 2)
    ssmc_st = jnp.swapaxes(state_ssm_conv, 1, 2)

    hp = x_prompt
    hs = x_sample.reshape(bs, D_MODEL)
    conv_p, ssm_p, ssmc_p = [], [], []
    conv_s = ssm_s = ssmc_s = None

    for i in range(depth):
        j = i // 2
        if i % 2 == 0:
            vp = _pw1_glu(hp.reshape(mp, D_MODEL), gm, w1, b1, i, j, tm_prompt, True)
            vp = vp.reshape(bp, seq, D_MODEL)
            tail = vp[:, seq - PERM_ROWS:, :].reshape(bp, PERM_VROWS, SUBLANES, D_MODEL)
            tail = jnp.swapaxes(tail, 1, 2).reshape(bp, PERM_ROWS, D_MODEL)
            conv_p.append(tail[:, PERM_ROWS - (CONV_WIDTH - 1):, :])
            hp = _conv_prompt(vp, hp, *cargs, j, tl_conv)
            vs = _pw1_glu(hs, gm, w1, b1, i, j, bs, False)
            hs, conv_s = _conv_step(vs, hs, conv_st, *cargs, j, 32, conv_s)
        else:
            hp, fin, ncs = _mamba_prompt(hp, gm, *margs, dsk, ng, wout, i, j, tl_ssm)
            ssm_p.append(fin.reshape(bp, N_HEADS, HEAD_DIM, D_STATE))
            ssmc_p.append(ncs)
            z, act, dt, da, ssmc_s = _mamba_step_in(hs, gm, *margs, ssmc_st, i, j, ssmc_s)
            ssm_s, y = _ssm_step(da[:, :N_HEADS], state_ssm, j, act, dt, dsk, 8, ssm_s)
            hs = _gnorm_out(y, z, hs, ng, wout, j)
        final = i == depth - 1
        hp = _ffn_ple(hp.reshape(mp, D_MODEL), pp, *fargs, i, tm_prompt, final)
        hp = hp.reshape(bp, seq, D_MODEL)
        hs = _ffn_ple(hs, ps, *fargs, i, bs, final)

    return (hp, hs.reshape(bs, 1, D_MODEL),
            jnp.stack(conv_p), jnp.stack(ssm_p), jnp.stack(ssmc_p),
            jnp.swapaxes(conv_s, 1, 2), ssm_s, jnp.swapaxes(ssmc_s, 1, 2))
```

```python
import functools

import jax
import jax.numpy as jnp
from jax import lax
from jax.experimental import pallas as pl
from jax.experimental.pallas import tpu as pltpu

F32 = jnp.float32
BF16 = jnp.bfloat16

D_MODEL = 1024
CONV_WIDTH = 31
D_INNER = 2048
HEAD_DIM = 64
N_HEADS = 32
N_GROUPS = 8
HEADS_PER_GROUP = 4
D_STATE = 128
SSM_CONV_WIDTH = 4
SSM_CONV_DIM = 4096
D_FF = 2816
PLE_DIM = 256
EPS = 1e-6

LANES = 128
SUBLANES = 8
GROUP_W = HEADS_PER_GROUP * HEAD_DIM
FF_CHUNK = 256
GLU_SLAB = 256
SSD_CHUNK = 128
PERM_ROWS = SSD_CHUNK
PERM_VROWS = PERM_ROWS // SUBLANES
SSM_EDGE = (SSM_CONV_WIDTH - 1) * SUBLANES
VMEM_LIMIT = 60 * 1024 * 1024


def _cparams(n_axes):
    return pltpu.CompilerParams(
        dimension_semantics=("arbitrary",) * n_axes, vmem_limit_bytes=VMEM_LIMIT)


def _sigmoid(x):
    return 1.0 / (1.0 + jnp.exp(-x))


def _silu(x):
    return x * _sigmoid(x)


def _softplus(x):
    return jnp.maximum(x, 0.0) + jnp.log(1.0 + jnp.exp(-jnp.abs(x)))


def _rms(x, g):
    return x * lax.rsqrt(jnp.mean(x * x, axis=-1, keepdims=True) + EPS) * g


def _dot(a, b):
    return jnp.dot(a, b, preferred_element_type=F32)


def _dot_nt(a, b):
    return lax.dot_general(a, b, (((1,), (1,)), ((), ())), preferred_element_type=F32)


def _dot_tn(a, b):
    return lax.dot_general(a, b, (((0,), (0,)), ((), ())), preferred_element_type=F32)


def _split3(x):
    hi = x.astype(BF16)
    r1 = x - hi.astype(F32)
    mid = r1.astype(BF16)
    lo = (r1 - mid.astype(F32)).astype(BF16)
    return hi, mid, lo


def _full(shape):
    return pl.BlockSpec(shape, lambda *_: (0,) * len(shape))


def _layer(shape, layer):
    return pl.BlockSpec((None,) + tuple(shape), lambda *_: (layer,) + (0,) * len(shape))


def _time_of_row(shape, axis):
    r = lax.broadcasted_iota(jnp.int32, shape, axis)
    return PERM_VROWS * (r & (SUBLANES - 1)) + lax.shift_right_logical(r, 3)


def _perm_matrix():
    col = lax.broadcasted_iota(jnp.int32, (PERM_ROWS, PERM_ROWS), 1)
    return (col == _time_of_row((PERM_ROWS, PERM_ROWS), 0)).astype(BF16)


def _unperm_matrix():
    row = lax.broadcasted_iota(jnp.int32, (PERM_ROWS, PERM_ROWS), 0)
    return (row == _time_of_row((PERM_ROWS, PERM_ROWS), 1)).astype(BF16)


def _reorder_rows(mat, x):
    blocks = [_dot(mat, x[c:c + PERM_ROWS, :]).astype(BF16) for c in range(0, x.shape[0], PERM_ROWS)]
    return blocks[0] if len(blocks) == 1 else jnp.concatenate(blocks, axis=0)


def _edge_vrow(cur, prv, m, sub):
    s = 1 if m <= PERM_VROWS else 2
    return jnp.where(sub < s, pltpu.roll(prv, s, 0), pltpu.roll(cur, s, 0))


def _pw1_glu_kernel(h_ref, g_ref, w_ref, b_ref, v_ref, *, permute):
    u = _rms(h_ref[...], g_ref[...]).astype(BF16)
    if permute:
        u = _reorder_rows(_perm_matrix(), u)
    for c0 in range(0, D_MODEL, GLU_SLAB):
        lin = _dot(u, w_ref[:, c0:c0 + GLU_SLAB]) + b_ref[:, c0:c0 + GLU_SLAB]
        gate = (_dot(u, w_ref[:, D_MODEL + c0:D_MODEL + c0 + GLU_SLAB])
                + b_ref[:, D_MODEL + c0:D_MODEL + c0 + GLU_SLAB])
        v_ref[:, c0:c0 + GLU_SLAB] = lin * _sigmoid(gate)


def _pw1_glu(h, g, w, b, mix_layer, conv_layer, tm, permute):
    m = h.shape[0]
    return pl.pallas_call(
        functools.partial(_pw1_glu_kernel, permute=permute),
        grid=(m // tm,),
        in_specs=[pl.BlockSpec((tm, D_MODEL), lambda i: (i, 0)),
                  _layer((1, D_MODEL), mix_layer), _layer((D_MODEL, 2 * D_MODEL), conv_layer),
                  _layer((1, 2 * D_MODEL), conv_layer)],
        out_specs=pl.BlockSpec((tm, D_MODEL), lambda i: (i, 0)),
        out_shape=jax.ShapeDtypeStruct((m, D_MODEL), F32),
        compiler_params=_cparams(1),
        name="pw1_glu",
    )(h, g, w, b)


def _conv_weight_specs(layer):
    vec = _layer((1, D_MODEL), layer)
    return [_layer((CONV_WIDTH, D_MODEL), layer), vec, vec, vec,
            _layer((D_MODEL, D_MODEL), layer), vec]


def _ln_silu(c, lng, lnb):
    mu = jnp.mean(c, axis=-1, keepdims=True)
    d = c - mu
    var = jnp.mean(d * d, axis=-1, keepdims=True)
    return _silu(d * lax.rsqrt(var + EPS) * lng + lnb)


def _conv_prompt_kernel(v_ref, h_ref, wdw_ref, bdw_ref, lng_ref, lnb_ref, w2_ref, b2_ref,
                        o_ref, xs, ext, wb, cbuf, *, tl):
    l = pl.program_id(1)
    nb = CONV_WIDTH - 1

    @pl.when((pl.program_id(0) == 0) & (l == 0))
    def _():
        for k in range(CONV_WIDTH):
            wb[k * SUBLANES:(k + 1) * SUBLANES, :] = jnp.broadcast_to(
                wdw_ref[k:k + 1, :], (SUBLANES, D_MODEL))

    @pl.when(l == 0)
    def _():
        xs[0:PERM_ROWS, :] = jnp.zeros((PERM_ROWS, D_MODEL), F32)

    @pl.when(l > 0)
    def _():
        xs[0:PERM_ROWS, :] = xs[tl:tl + PERM_ROWS, :]

    xs[PERM_ROWS:PERM_ROWS + tl, :] = v_ref[...]
    sub = lax.broadcasted_iota(jnp.int32, (SUBLANES, D_MODEL), 0)
    lw = 2 * LANES

    def block(c, carry):
        r0 = pl.multiple_of(c * PERM_ROWS, PERM_ROWS)
        for m in range(1, nb + 1):
            jm = (PERM_VROWS if m <= PERM_VROWS else 2 * PERM_VROWS) - m
            cur = xs[pl.ds(r0 + PERM_ROWS + SUBLANES * jm, SUBLANES), :]
            prv = xs[pl.ds(r0 + SUBLANES * jm, SUBLANES), :]
            ext[(nb - m) * SUBLANES:(nb - m + 1) * SUBLANES, :] = _edge_vrow(cur, prv, m, sub)
        ext[nb * SUBLANES:nb * SUBLANES + PERM_ROWS, :] = xs[pl.ds(r0 + PERM_ROWS, PERM_ROWS), :]
        for lb in range(0, D_MODEL, lw):
            accs = [jnp.broadcast_to(bdw_ref[:, lb:lb + lw], (SUBLANES, lw))] * PERM_VROWS
            for k in range(CONV_WIDTH):
                w8 = wb[k * SUBLANES:(k + 1) * SUBLANES, lb:lb + lw]
                accs = [a + ext[(j + k) * SUBLANES:(j + k + 1) * SUBLANES, lb:lb + lw] * w8
                        for j, a in enumerate(accs)]
            for j, a in enumerate(accs):
                cbuf[pl.ds(r0 + j * SUBLANES, SUBLANES), lb:lb + lw] = a
        return carry

    lax.fori_loop(0, tl // PERM_ROWS, block, 0)
    y = _ln_silu(cbuf[...], lng_ref[...], lnb_ref[...]).astype(BF16)
    y = _reorder_rows(_unperm_matrix(), y)
    o_ref[...] = h_ref[...] + _dot(y, w2_ref[...]) + b2_ref[...]


def _conv_prompt(v, h, wdw, bdw, lng, lnb, w2, b2, layer, tl):
    bsz, seq, _ = v.shape
    tile = pl.BlockSpec((None, tl, D_MODEL), lambda b, l: (b, l, 0))
    return pl.pallas_call(
        functools.partial(_conv_prompt_kernel, tl=tl),
        grid=(bsz, seq // tl),
        in_specs=[tile, tile] + _conv_weight_specs(layer),
        out_specs=tile,
        out_shape=jax.ShapeDtypeStruct(v.shape, F32),
        scratch_shapes=[pltpu.VMEM((tl + PERM_ROWS, D_MODEL), F32),
                        pltpu.VMEM(((CONV_WIDTH - 1) * SUBLANES + PERM_ROWS, D_MODEL), F32),
                        pltpu.VMEM((CONV_WIDTH * SUBLANES, D_MODEL), F32),
                        pltpu.VMEM((tl, D_MODEL), F32)],
        compiler_params=_cparams(2),
        name="conv_prompt",
    )(v, h, wdw, bdw, lng, lnb, w2, b2)


def _stack_plan(layer, n_layers, nb, fill):
    park = 0 if layer > 0 else nb - 1
    if fill:
        return (n_layers, nb), (lambda ls, i: jnp.where(ls == layer, i, park)), (lambda ls: ls)
    return (1, nb), (lambda ls, i: i), (lambda ls: layer)


def _own_slot_or_zero(fill, layer, state_out_ref, body):
    if not fill:
        body()
        return

    @pl.when(pl.program_id(0) != layer)
    def _():
        state_out_ref[...] = jnp.zeros(state_out_ref.shape, F32)

    pl.when(pl.program_id(0) == layer)(body)


def _conv_step_kernel(v_ref, h_ref, st_ref, wdw_ref, bdw_ref, lng_ref, lnb_ref, w2_ref, b2_ref,
                      *rest, layer, fill):
    o_ref, nst_ref = rest[-2:]
    nk = CONV_WIDTH - 1

    def body():
        v = v_ref[...]
        acc = bdw_ref[...] + v * wdw_ref[nk:nk + 1, :]
        for k in range(nk):
            acc = acc + st_ref[k] * wdw_ref[k:k + 1, :]
        nst_ref[0:nk - 1] = st_ref[1:nk]
        nst_ref[nk - 1] = v
        y = _ln_silu(acc, lng_ref[...], lnb_ref[...]).astype(BF16)
        o_ref[...] = h_ref[...] + _dot(y, w2_ref[...]) + b2_ref[...]

    _own_slot_or_zero(fill, layer, nst_ref, body)


def _conv_step(v, h, states, wdw, bdw, lng, lnb, w2, b2, layer, tb, prev_out):
    n_layers, _, bsz, _ = states.shape
    fill = prev_out is None
    grid, bi, li = _stack_plan(layer, n_layers, bsz // tb, fill)
    row = pl.BlockSpec((tb, D_MODEL), lambda ls, i: (bi(ls, i), 0))
    sblk = (None, CONV_WIDTH - 1, tb, D_MODEL)
    in_specs = [row, row, pl.BlockSpec(sblk, lambda ls, i: (layer, 0, bi(ls, i), 0))]
    in_specs += _conv_weight_specs(layer)
    args = [v, h, states, wdw, bdw, lng, lnb, w2, b2]
    aliases = {}
    if not fill:
        in_specs.append(pl.BlockSpec(memory_space=pl.ANY))
        args.append(prev_out)
        aliases = {len(args) - 1: 1}
    return pl.pallas_call(
        functools.partial(_conv_step_kernel, layer=layer, fill=fill),
        grid=grid,
        in_specs=in_specs,
        out_specs=[row, pl.BlockSpec(sblk, lambda ls, i: (li(ls), 0, i, 0))],
        out_shape=[jax.ShapeDtypeStruct((bsz, D_MODEL), F32),
                   jax.ShapeDtypeStruct(states.shape, F32)],
        input_output_aliases=aliases,
        compiler_params=_cparams(2),
        name="conv_step",
    )(*args)


def _ffn_ple_kernel(h_ref, p_ref, gf_ref, wg_ref, wu_ref, wd_ref, gp_ref, wpg_ref, wpp_ref,
                    gfin_ref, o_ref, *, final):
    h = h_ref[...]
    u = _rms(h, gf_ref[...]).astype(BF16)
    acc = h
    for f in range(0, D_FF, FF_CHUNK):
        gate = _dot(u, wg_ref[:, f:f + FF_CHUNK])
        up = _dot(u, wu_ref[:, f:f + FF_CHUNK])
        a = (_silu(gate) * up).astype(BF16)
        acc = acc + _dot(a, wd_ref[f:f + FF_CHUNK, :])
    pg = _sigmoid(_dot(_rms(acc, gp_ref[...]).astype(BF16), wpg_ref[...]))
    out = acc + _dot(p_ref[...].astype(BF16), wpp_ref[...]) * pg
    if final:
        out = _rms(out, gfin_ref[...])
    o_ref[...] = out


def _ffn_ple(h, p, gf, wg, wu, wd, gp, wpg, wpp, gfin, layer, tm, final):
    m = h.shape[0]
    vec = _layer((1, D_MODEL), layer)
    return pl.pallas_call(
        functools.partial(_ffn_ple_kernel, final=final),
        grid=(m // tm,),
        in_specs=[pl.BlockSpec((tm, D_MODEL), lambda i: (i, 0)),
                  pl.BlockSpec((None, tm, PLE_DIM), lambda i: (layer, i, 0)),
                  vec, _layer((D_MODEL, D_FF), layer), _layer((D_MODEL, D_FF), layer),
                  _layer((D_FF, D_MODEL), layer), vec, _layer((D_MODEL, D_MODEL), layer),
                  _layer((PLE_DIM, D_MODEL), layer), _full((1, D_MODEL))],
        out_specs=pl.BlockSpec((tm, D_MODEL), lambda i: (i, 0)),
        out_shape=jax.ShapeDtypeStruct((m, D_MODEL), F32),
        compiler_params=_cparams(1),
        name="ffn_ple",
    )(h, p, gf, wg, wu, wd, gp, wpg, wpp, gfin)


def _gated_norm(y, gate, ng):
    yg = y * gate
    parts = []
    for g in range(N_GROUPS):
        blk = yg[:, g * GROUP_W:(g + 1) * GROUP_W]
        ms = jnp.mean(blk * blk, axis=-1, keepdims=True)
        parts.append(blk * lax.rsqrt(ms + EPS))
    return (jnp.concatenate(parts, axis=-1) * ng).astype(BF16)


def _lane_block_ids(shape, width):
    return lax.broadcasted_iota(jnp.int32, shape, 1) // width


def _expand_heads(v, g, rows):
    blk = _lane_block_ids((rows, GROUP_W), HEAD_DIM)
    out = jnp.zeros((rows, GROUP_W), F32)
    for r in range(HEADS_PER_GROUP):
        hh = g * HEADS_PER_GROUP + r
        col = jnp.broadcast_to(v[:, hh:hh + 1], (rows, GROUP_W))
        out = jnp.where(blk == r, col, out)
    return out


def _mamba_prompt_kernel(hn_ref, hc_ref, g_ref, wzx_ref, wdt_ref, dtb_ref, alog_ref, wcv_ref,
                         bcv_ref, dsk_ref, ng_ref, wout_ref,
                         o_ref, st_ref, cs_ref,
                         z_scr, xb_scr, dt_scr, u_scr, gz_scr, dtc_scr, x_scr, b_scr, c_scr, y_scr,
                         s_scr, *, tl, nt):
    s = pl.program_id(0)
    q = SSD_CHUNK
    nk = SSM_CONV_WIDTH - 1
    first_in_seq = lax.rem(jnp.maximum(s - 1, 0), nt) == 0
    next_starts_seq = lax.rem(s, nt) == 0

    @pl.when(s == 0)
    def _():
        z_scr[...] = jnp.zeros(z_scr.shape, F32)
        xb_scr[...] = jnp.zeros(xb_scr.shape, F32)
        dt_scr[...] = jnp.zeros(dt_scr.shape, F32)
        s_scr[...] = jnp.zeros(s_scr.shape, F32)

    gz_scr[...] = _silu(z_scr[...])
    dtc_scr[...] = dt_scr[...]
    for k in range(nk):
        r = tl + SSM_EDGE - (nk - k) * SUBLANES + SUBLANES - 1
        cs_ref[k:k + 1, :] = xb_scr[r:r + 1, :]

    cw = 4 * LANES
    sub = lax.broadcasted_iota(jnp.int32, (SUBLANES, cw), 0)
    for cb0 in range(0, SSM_CONV_DIM, cw):
        cl = slice(cb0, cb0 + cw)
        for c in range(tl // q):
            base = c * q
            cur = xb_scr[base + SSM_EDGE:base + SSM_EDGE + q, cl]
            edges = []
            for m in range(nk, 0, -1):
                jm = PERM_VROWS - m
                cur_v = cur[jm * SUBLANES:(jm + 1) * SUBLANES, :]
                prv_v = xb_scr[base + (nk - m) * SUBLANES:base + (nk - m + 1) * SUBLANES, cl]
                edges.append(_edge_vrow(cur_v, prv_v, m, sub))
            hist = jnp.concatenate(edges + [cur], axis=0)
            conv = bcv_ref[:, cl] + cur * wcv_ref[nk:nk + 1, cl]
            for s in range(1, SSM_CONV_WIDTH):
                lo = SSM_EDGE - s * SUBLANES
                conv = conv + hist[lo:lo + q, :] * wcv_ref[nk - s:nk - s + 1, cl]
            act = _silu(conv)
            rows = slice(base, base + q)
            if cb0 < D_INNER:
                x_scr[rows, cl] = act
            elif cb0 < D_INNER + N_GROUPS * D_STATE:
                b_scr[rows, cb0 - D_INNER:cb0 - D_INNER + cw] = act.astype(BF16)
            else:
                c0 = cb0 - D_INNER - N_GROUPS * D_STATE
                c_scr[rows, c0:c0 + cw] = act.astype(BF16)

    u_scr[...] = _reorder_rows(_perm_matrix(), _rms(hn_ref[...], g_ref[...]).astype(BF16))
    dt_scr[...] = _softplus(_dot(u_scr[...], wdt_ref[...]) + dtb_ref[...])
    slab_w = 2 * LANES
    slabs = list(range(0, D_INNER + SSM_CONV_DIM, slab_w))

    def emit_in_proj_slab():
        if not slabs:
            return
        c0 = slabs.pop(0)
        res = _dot(u_scr[...], wzx_ref[:, c0:c0 + slab_w])
        if c0 < D_INNER:
            z_scr[:, c0:c0 + slab_w] = res
        else:
            xc = slice(c0 - D_INNER, c0 - D_INNER + slab_w)
            edge = jnp.where(next_starts_seq, 0.0, xb_scr[tl:tl + SSM_EDGE, xc])
            xb_scr[SSM_EDGE:SSM_EDGE + tl, xc] = res
            xb_scr[0:SSM_EDGE, xc] = edge

    a_neg = -jnp.exp(alog_ref[...])
    causal = _time_of_row((q, q), 0) >= _time_of_row((q, q), 1)
    tril = causal.astype(BF16)
    low_half = lax.broadcasted_iota(jnp.int32, (q, LANES), 1) < HEAD_DIM
    blk = _lane_block_ids((q, GROUP_W), HEAD_DIM)

    def pair_expand(cols):
        return jnp.concatenate([jnp.where(low_half, cols[0], cols[1]),
                                jnp.where(low_half, cols[2], cols[3])], axis=-1)

    for c in range(tl // q):
        rows = slice(c * q, (c + 1) * q)
        dtc = dtc_scr[rows, :]
        a = dtc * a_neg
        a_hi, a_mid, a_lo = _split3(a)
        a_cum = _dot(tril, a_hi) + _dot(tril, a_mid) + _dot(tril, a_lo)
        a_cum_t = a_cum.T
        for g in range(N_GROUPS):
            gs = slice(g * GROUP_W, (g + 1) * GROUP_W)
            ns = slice(g * D_STATE, (g + 1) * D_STATE)
            heads = [g * HEADS_PER_GROUP + r for r in range(HEADS_PER_GROUP)]
            bg = b_scr[rows, ns]
            cg = c_scr[rows, ns]
            xg = x_scr[rows, gs]
            sg = s_scr[gs, :]
            if c == 0:
                sg = jnp.where(first_in_seq, 0.0, sg)
            cb = _dot_nt(cg, bg)
            y_off = _dot_nt(cg, sg.astype(BF16))
            bc_a = [jnp.broadcast_to(a_cum[:, hh:hh + 1], (q, LANES)) for hh in heads]
            bc_dt = [jnp.broadcast_to(dtc[:, hh:hh + 1], (q, LANES)) for hh in heads]
            a_e = pair_expand(bc_a)
            xdt = xg * pair_expand(bc_dt)
            ms = []
            for r, hh in enumerate(heads):
                seg = bc_a[r] - a_cum_t[hh:hh + 1, :]
                lm = jnp.exp(jnp.where(causal, seg, -jnp.inf))
                ms.append((cb * lm).astype(BF16))
            xblk = jnp.concatenate(
                [jnp.where(blk == r, xdt, 0.0).astype(BF16) for r in range(HEADS_PER_GROUP)], axis=0)
            y = (_dot(jnp.concatenate(ms, axis=-1), xblk) + y_off * jnp.exp(a_e)
                 + xg * dsk_ref[:, gs])
            y_scr[rows, gs] = y
            xs = (xdt * jnp.exp(a_e[q - 1:q, :] - a_e)).astype(BF16)
            add = _dot_tn(xs, bg)
            for r, hh in enumerate(heads):
                cd = jnp.exp(bc_a[r][q - 1:q, :])
                hs = slice(hh * HEAD_DIM, (hh + 1) * HEAD_DIM)
                ps = slice(r * HEAD_DIM, (r + 1) * HEAD_DIM)
                s_scr[hs, :] = cd * sg[ps, :] + add[ps, :]
            emit_in_proj_slab()
            if c == 0:
                emit_in_proj_slab()

    while slabs:
        emit_in_proj_slab()
    st_ref[...] = s_scr[...]
    yn = _reorder_rows(_unperm_matrix(), _gated_norm(y_scr[...], gz_scr[...], ng_ref[...]))
    o_ref[...] = hc_ref[...] + _dot(yn, wout_ref[...])


def _mamba_weight_specs(mix_layer, layer):
    return [_layer((1, D_MODEL), mix_layer),
            pl.BlockSpec((None, D_MODEL, D_INNER + SSM_CONV_DIM), lambda *_: (layer, 0, 0)),
            _layer((D_MODEL, LANES), layer), _layer((1, LANES), layer), _layer((1, LANES), layer),
            _layer((SSM_CONV_WIDTH, SSM_CONV_DIM), layer), _layer((1, SSM_CONV_DIM), layer)]


def _mamba_prompt(h, g, wzx, wdt, dtb, alog, wcv, bcv, dsk, ng, wout, mix_layer, layer, tl):
    bsz, seq, _ = h.shape
    nt = seq // tl
    n_tiles = bsz * nt
    nconv = SSM_CONV_WIDTH - 1

    def cur(s):
        return jnp.maximum(s - 1, 0)

    def nxt(s):
        return jnp.minimum(s, n_tiles - 1)

    tile_cur = pl.BlockSpec((None, tl, D_MODEL), lambda s: (cur(s) // nt, cur(s) % nt, 0))
    tile_nxt = pl.BlockSpec((None, tl, D_MODEL), lambda s: (nxt(s) // nt, nxt(s) % nt, 0))
    return pl.pallas_call(
        functools.partial(_mamba_prompt_kernel, tl=tl, nt=nt),
        grid=(n_tiles + 1,),
        in_specs=[tile_nxt, tile_cur] + _mamba_weight_specs(mix_layer, layer) + [
            _layer((1, D_INNER), layer), _layer((1, D_INNER), layer),
            _layer((D_INNER, D_MODEL), layer)],
        out_specs=[tile_cur,
                   pl.BlockSpec((None, D_INNER, D_STATE), lambda s: (cur(s) // nt, 0, 0)),
                   pl.BlockSpec((None, nconv, SSM_CONV_DIM), lambda s: (cur(s) // nt, 0, 0))],
        out_shape=[jax.ShapeDtypeStruct(h.shape, F32),
                   jax.ShapeDtypeStruct((bsz, D_INNER, D_STATE), F32),
                   jax.ShapeDtypeStruct((bsz, nconv, SSM_CONV_DIM), F32)],
        scratch_shapes=[pltpu.VMEM((tl, D_INNER), F32),
                        pltpu.VMEM((tl + SSM_EDGE, SSM_CONV_DIM), F32),
                        pltpu.VMEM((tl, LANES), F32),
                        pltpu.VMEM((tl, D_MODEL), BF16),
                        pltpu.VMEM((tl, D_INNER), F32),
                        pltpu.VMEM((tl, LANES), F32),
                        pltpu.VMEM((tl, D_INNER), F32),
                        pltpu.VMEM((tl, N_GROUPS * D_STATE), BF16),
                        pltpu.VMEM((tl, N_GROUPS * D_STATE), BF16),
                        pltpu.VMEM((tl, D_INNER), F32),
                        pltpu.VMEM((D_INNER, D_STATE), F32)],
        compiler_params=_cparams(1),
        name="mamba_prompt",
    )(h, h, g, wzx, wdt, dtb, alog, wcv, bcv, dsk, ng, wout)


def _mamba_step_in_kernel(h_ref, g_ref, wzx_ref, wdt_ref, dtb_ref, alog_ref, wcv_ref, bcv_ref,
                          cs_ref, *rest, layer, fill):
    z_ref, act_ref, dt_ref, da_ref, ncs_ref = rest[-5:]
    nk = SSM_CONV_WIDTH - 1

    def body():
        u = _rms(h_ref[...], g_ref[...]).astype(BF16)
        z_ref[...] = _dot(u, wzx_ref[:, :D_INNER])
        xb = _dot(u, wzx_ref[:, D_INNER:])
        dt = _softplus(_dot(u, wdt_ref[...]) + dtb_ref[...])
        dt_ref[...] = dt
        da_ref[...] = jnp.exp(dt * (-jnp.exp(alog_ref[...])))
        conv = bcv_ref[...] + xb * wcv_ref[nk:nk + 1, :]
        for k in range(nk):
            conv = conv + cs_ref[k] * wcv_ref[k:k + 1, :]
        act_ref[...] = _silu(conv)
        ncs_ref[0:nk - 1] = cs_ref[1:nk]
        ncs_ref[nk - 1] = xb

    _own_slot_or_zero(fill, layer, ncs_ref, body)


def _mamba_step_in(h, g, wzx, wdt, dtb, alog, wcv, bcv, states, mix_layer, layer, prev_out):
    n_layers, nk, bsz, _ = states.shape
    fill = prev_out is None
    grid, _, li = _stack_plan(layer, n_layers, 1, fill)
    sblk = (None, nk, bsz, SSM_CONV_DIM)
    shapes = [(bsz, D_INNER), (bsz, SSM_CONV_DIM), (bsz, LANES), (bsz, LANES)]
    in_specs = [_full((bsz, D_MODEL))] + _mamba_weight_specs(mix_layer, layer)
    in_specs.append(pl.BlockSpec(sblk, lambda ls, i: (layer, 0, 0, 0)))
    args = [h, g, wzx, wdt, dtb, alog, wcv, bcv, states]
    aliases = {}
    if not fill:
        in_specs.append(pl.BlockSpec(memory_space=pl.ANY))
        args.append(prev_out)
        aliases = {len(args) - 1: len(shapes)}
    return pl.pallas_call(
        functools.partial(_mamba_step_in_kernel, layer=layer, fill=fill),
        grid=grid,
        in_specs=in_specs,
        out_specs=[_full(s) for s in shapes] + [pl.BlockSpec(sblk, lambda ls, i: (li(ls), 0, 0, 0))],
        out_shape=[jax.ShapeDtypeStruct(s, F32) for s in shapes]
        + [jax.ShapeDtypeStruct(states.shape, F32)],
        input_output_aliases=aliases,
        compiler_params=_cparams(2),
        name="mamba_step_in",
    )(*args)


def _ssm_step_kernel(da_ref, s_ref, act_ref, dt_ref, dsk_ref, *rest, tb, layer, fill):
    ns_ref, y_ref = rest[-2:]
    _own_slot_or_zero(
        fill, layer, ns_ref,
        lambda: _ssm_step_body(da_ref, s_ref, act_ref, dt_ref, dsk_ref, ns_ref, y_ref, tb))


def _ssm_step_body(da_ref, s_ref, act_ref, dt_ref, dsk_ref, ns_ref, y_ref, tb):
    i = pl.program_id(1)
    rid = lax.broadcasted_iota(jnp.int32, (tb, 1), 0)
    zpad = jnp.zeros((tb, GROUP_W), F32)
    dt = dt_ref[...]
    for g in range(N_GROUPS):
        gs = slice(g * GROUP_W, (g + 1) * GROUP_W)
        xg = act_ref[:, gs]
        bg = act_ref[:, D_INNER + g * D_STATE:D_INNER + (g + 1) * D_STATE]
        cg = act_ref[:, D_INNER + (N_GROUPS + g) * D_STATE:D_INNER + (N_GROUPS + g + 1) * D_STATE]
        xdt = xg * _expand_heads(dt, g, tb)
        y8 = jnp.zeros((tb, GROUP_W), F32)
        for r in range(tb):
            sel = rid == r
            x1 = jnp.concatenate([jnp.where(sel, xdt, 0.0), zpad], axis=0).astype(BF16)
            b1 = jnp.concatenate([jnp.where(sel, bg, 0.0), zpad[:, :D_STATE]], axis=0).astype(BF16)
            c1 = jnp.concatenate([jnp.where(sel, cg, 0.0), zpad[:, :D_STATE]], axis=0).astype(BF16)
            add = _dot_tn(x1, b1)
            news = []
            for k in range(HEADS_PER_GROUP):
                hh = g * HEADS_PER_GROUP + k
                da = da_ref[i * tb + r, hh]
                sn = da * s_ref[r, hh] + add[k * HEAD_DIM:(k + 1) * HEAD_DIM, :]
                ns_ref[r, hh] = sn
                news.append(sn)
            sg = jnp.concatenate(news, axis=0).astype(BF16)
            y8 = y8 + _dot_nt(c1, sg)[:tb, :]
        y_ref[:, gs] = y8 + xg * dsk_ref[:, gs]


def _ssm_step(da, states, layer, act, dt, dsk, tb, prev_out):
    n_layers, bsz = states.shape[:2]
    fill = prev_out is None
    grid, bi, li = _stack_plan(layer, n_layers, bsz // tb, fill)
    sblk = (None, tb, N_HEADS, HEAD_DIM, D_STATE)
    in_specs = [pl.BlockSpec(memory_space=pltpu.SMEM),
                pl.BlockSpec(sblk, lambda ls, i: (layer, bi(ls, i), 0, 0, 0)),
                pl.BlockSpec((tb, SSM_CONV_DIM), lambda ls, i: (bi(ls, i), 0)),
                pl.BlockSpec((tb, LANES), lambda ls, i: (bi(ls, i), 0)),
                _layer((1, D_INNER), layer)]
    args = [da, states, act, dt, dsk]
    aliases = {}
    if not fill:
        in_specs.append(pl.BlockSpec(memory_space=pl.ANY))
        args.append(prev_out)
        aliases = {len(args) - 1: 0}
    return pl.pallas_call(
        functools.partial(_ssm_step_kernel, tb=tb, layer=layer, fill=fill),
        grid=grid,
        in_specs=in_specs,
        out_specs=[pl.BlockSpec(sblk, lambda ls, i: (li(ls), i, 0, 0, 0)),
                   pl.BlockSpec((tb, D_INNER), lambda ls, i: (bi(ls, i), 0))],
        out_shape=[jax.ShapeDtypeStruct(states.shape, F32),
                   jax.ShapeDtypeStruct((bsz, D_INNER), F32)],
        input_output_aliases=aliases,
        compiler_params=_cparams(2),
        name="ssm_step",
    )(*args)


def _gnorm_out_kernel(y_ref, z_ref, h_ref, ng_ref, wout_ref, o_ref):
    yn = _gated_norm(y_ref[...], _silu(z_ref[...]), ng_ref[...])
    o_ref[...] = h_ref[...] + _dot(yn, wout_ref[...])


def _gnorm_out(y, z, h, ng, wout, layer):
    bsz = y.shape[0]
    return pl.pallas_call(
        _gnorm_out_kernel,
        grid=(1,),
        in_specs=[_full((bsz, D_INNER)), _full((bsz, D_INNER)), _full((bsz, D_MODEL)),
                  _layer((1, D_INNER), layer), _layer((D_INNER, D_MODEL), layer)],
        out_specs=_full((bsz, D_MODEL)),
        out_shape=jax.ShapeDtypeStruct((bsz, D_MODEL), F32),
        compiler_params=_cparams(1),
        name="gnorm_out",
    )(y, z, h, ng, wout)


def _rows(x):
    return x.reshape(x.shape[0], 1, x.shape[1])


def _pad_lanes(x):
    return jnp.pad(x, [(0, 0)] * (x.ndim - 1) + [(0, LANES - x.shape[-1])])


def kernel(x_prompt, x_sample, p_prompt, p_sample, state_conv_mixer, state_ssm, state_ssm_conv, g_mix, g_ffn, g_ple, g_final, cv_w_pw1, cv_b_pw1, cv_w_dw, cv_b_dw, cv_ln_g, cv_ln_b, cv_w_pw2, cv_b_pw2, ssm_w_in, ssm_w_conv, ssm_b_conv, ssm_dt_bias, ssm_a_log, ssm_d, ssm_norm_g, ssm_w_out, ffn_w_gate, ffn_w_up, ffn_w_down, ple_w_proj, ple_w_gate):
    depth = g_mix.shape[0]
    bp, seq, _ = x_prompt.shape
    bs = x_sample.shape[0]
    mp = bp * seq
    tm_prompt = 512
    tm_ffn = 1024
    tl_conv = 512
    tl_ssm = 256
    zx_w = D_INNER + SSM_CONV_DIM

    gm, gf, gp = _rows(g_mix), _rows(g_ffn), _rows(g_ple)
    gfin = g_final.reshape(1, D_MODEL)
    w1, b1 = cv_w_pw1.astype(BF16), _rows(cv_b_pw1)
    cargs = (cv_w_dw, _rows(cv_b_dw), _rows(cv_ln_g), _rows(cv_ln_b),
             cv_w_pw2.astype(BF16), _rows(cv_b_pw2))
    w_in = ssm_w_in.astype(BF16)
    margs = (w_in, _pad_lanes(ssm_w_in[:, :, zx_w:]).astype(BF16), _rows(_pad_lanes(ssm_dt_bias)),
             _rows(_pad_lanes(ssm_a_log)), ssm_w_conv, _rows(ssm_b_conv))
    dsk = _rows(jnp.repeat(ssm_d, HEAD_DIM, axis=1))
    ng = _rows(ssm_norm_g)
    wout = ssm_w_out.astype(BF16)
    fargs = (gf, ffn_w_gate.astype(BF16), ffn_w_up.astype(BF16), ffn_w_down.astype(BF16), gp,
             ple_w_gate.astype(BF16), ple_w_proj.astype(BF16), gfin)
    pp = p_prompt.reshape(depth, mp, PLE_DIM)
    ps = p_sample.reshape(depth, bs, PLE_DIM)
    conv_st = jnp.swapaxes(state_conv_mixer, 1, 2)
    ssmc_st = jnp.swapaxes(state_ssm_conv, 1, 2)

    hp = x_prompt
    hs = x_sample.reshape(bs, D_MODEL)
    conv_p, ssm_p, ssmc_p = [], [], []
    conv_s = ssm_s = ssmc_s = None

    for i in range(depth):
        j = i // 2
        if i % 2 == 0:
            vp = _pw1_glu(hp.reshape(mp, D_MODEL), gm, w1, b1, i, j, tm_prompt, True)
            vp = vp.reshape(bp, seq, D_MODEL)
            tail = vp[:, seq - PERM_ROWS:, :].reshape(bp, PERM_VROWS, SUBLANES, D_MODEL)
            tail = jnp.swapaxes(tail, 1, 2).reshape(bp, PERM_ROWS, D_MODEL)
            conv_p.append(tail[:, PERM_ROWS - (CONV_WIDTH - 1):, :])
            hp = _conv_prompt(vp, hp, *cargs, j, tl_conv)
            vs = _pw1_glu(hs, gm, w1, b1, i, j, bs, False)
            hs, conv_s = _conv_step(vs, hs, conv_st, *cargs, j, 32, conv_s)
        else:
            hp, fin, ncs = _mamba_prompt(hp, gm, *margs, dsk, ng, wout, i, j, tl_ssm)
            ssm_p.append(fin.reshape(bp, N_HEADS, HEAD_DIM, D_STATE))
            ssmc_p.append(ncs)
            z, act, dt, da, ssmc_s = _mamba_step_in(hs, gm, *margs, ssmc_st, i, j, ssmc_s)
            ssm_s, y = _ssm_step(da[:, :N_HEADS], state_ssm, j, act, dt, dsk, 8, ssm_s)
            hs = _gnorm_out(y, z, hs, ng, wout, j)
        final = i == depth - 1
        hp = _ffn_ple(hp.reshape(mp, D_MODEL), pp, *fargs, i, tm_ffn, final)
        hp = hp.reshape(bp, seq, D_MODEL)
        hs = _ffn_ple(hs, ps, *fargs, i, bs, final)

    return (hp, hs.reshape(bs, 1, D_MODEL),
            jnp.stack(conv_p), jnp.stack(ssm_p), jnp.stack(ssmc_p),
            jnp.swapaxes(conv_s, 1, 2), ssm_s, jnp.swapaxes(ssmc_s, 1, 2))
```

```python
import functools

import jax
import jax.numpy as jnp
from jax import lax
from jax.experimental import pallas as pl
from jax.experimental.pallas import tpu as pltpu

F32 = jnp.float32
BF16 = jnp.bfloat16

D_MODEL = 1024
CONV_WIDTH = 31
D_INNER = 2048
HEAD_DIM = 64
N_HEADS = 32
N_GROUPS = 8
HEADS_PER_GROUP = 4
D_STATE = 128
SSM_CONV_WIDTH = 4
SSM_CONV_DIM = 4096
D_FF = 2816
PLE_DIM = 256
EPS = 1e-6

LANES = 128
SUBLANES = 8
GROUP_W = HEADS_PER_GROUP * HEAD_DIM
FF_CHUNK = 256
GLU_SLAB = 256
SSD_CHUNK = 128
PERM_ROWS = SSD_CHUNK
PERM_VROWS = PERM_ROWS // SUBLANES
SSM_EDGE = (SSM_CONV_WIDTH - 1) * SUBLANES
VMEM_LIMIT = 60 * 1024 * 1024


def _cparams(n_axes):
    return pltpu.CompilerParams(
        dimension_semantics=("arbitrary",) * n_axes, vmem_limit_bytes=VMEM_LIMIT)


def _sigmoid(x):
    return 1.0 / (1.0 + jnp.exp(-x))


def _silu(x):
    return x * _sigmoid(x)


def _softplus(x):
    return jnp.maximum(x, 0.0) + jnp.log(1.0 + jnp.exp(-jnp.abs(x)))


def _rms(x, g):
    return x * lax.rsqrt(jnp.mean(x * x, axis=-1, keepdims=True) + EPS) * g


def _dot(a, b):
    return jnp.dot(a, b, preferred_element_type=F32)


def _dot_nt(a, b):
    return lax.dot_general(a, b, (((1,), (1,)), ((), ())), preferred_element_type=F32)


def _dot_tn(a, b):
    return lax.dot_general(a, b, (((0,), (0,)), ((), ())), preferred_element_type=F32)


def _split3(x):
    hi = x.astype(BF16)
    r1 = x - hi.astype(F32)
    mid = r1.astype(BF16)
    lo = (r1 - mid.astype(F32)).astype(BF16)
    return hi, mid, lo


def _full(shape):
    return pl.BlockSpec(shape, lambda *_: (0,) * len(shape))


def _layer(shape, layer):
    return pl.BlockSpec((None,) + tuple(shape), lambda *_: (layer,) + (0,) * len(shape))


def _time_of_row(shape, axis):
    r = lax.broadcasted_iota(jnp.int32, shape, axis)
    return PERM_VROWS * (r & (SUBLANES - 1)) + lax.shift_right_logical(r, 3)


def _perm_matrix():
    col = lax.broadcasted_iota(jnp.int32, (PERM_ROWS, PERM_ROWS), 1)
    return (col == _time_of_row((PERM_ROWS, PERM_ROWS), 0)).astype(BF16)


def _unperm_matrix():
    row = lax.broadcasted_iota(jnp.int32, (PERM_ROWS, PERM_ROWS), 0)
    return (row == _time_of_row((PERM_ROWS, PERM_ROWS), 1)).astype(BF16)


def _reorder_rows(mat, x):
    blocks = [_dot(mat, x[c:c + PERM_ROWS, :]).astype(BF16) for c in range(0, x.shape[0], PERM_ROWS)]
    return blocks[0] if len(blocks) == 1 else jnp.concatenate(blocks, axis=0)


def _edge_vrow(cur, prv, m, sub):
    s = 1 if m <= PERM_VROWS else 2
    return jnp.where(sub < s, pltpu.roll(prv, s, 0), pltpu.roll(cur, s, 0))


def _pw1_glu_kernel(h_ref, g_ref, w_ref, b_ref, v_ref, *, permute):
    u = _rms(h_ref[...], g_ref[...]).astype(BF16)
    if permute:
        u = _reorder_rows(_perm_matrix(), u)
    for c0 in range(0, D_MODEL, GLU_SLAB):
        lin = _dot(u, w_ref[:, c0:c0 + GLU_SLAB]) + b_ref[:, c0:c0 + GLU_SLAB]
        gate = (_dot(u, w_ref[:, D_MODEL + c0:D_MODEL + c0 + GLU_SLAB])
                + b_ref[:, D_MODEL + c0:D_MODEL + c0 + GLU_SLAB])
        v_ref[:, c0:c0 + GLU_SLAB] = lin * _sigmoid(gate)


def _pw1_glu(h, g, w, b, mix_layer, conv_layer, tm, permute):
    m = h.shape[0]
    return pl.pallas_call(
        functools.partial(_pw1_glu_kernel, permute=permute),
        grid=(m // tm,),
        in_specs=[pl.BlockSpec((tm, D_MODEL), lambda i: (i, 0)),
                  _layer((1, D_MODEL), mix_layer), _layer((D_MODEL, 2 * D_MODEL), conv_layer),
                  _layer((1, 2 * D_MODEL), conv_layer)],
        out_specs=pl.BlockSpec((tm, D_MODEL), lambda i: (i, 0)),
        out_shape=jax.ShapeDtypeStruct((m, D_MODEL), F32),
        compiler_params=_cparams(1),
        name="pw1_glu",
    )(h, g, w, b)


def _conv_weight_specs(layer):
    vec = _layer((1, D_MODEL), layer)
    return [_layer((CONV_WIDTH, D_MODEL), layer), vec, vec, vec,
            _layer((D_MODEL, D_MODEL), layer), vec]


def _ln_silu(c, lng, lnb):
    mu = jnp.mean(c, axis=-1, keepdims=True)
    d = c - mu
    var = jnp.mean(d * d, axis=-1, keepdims=True)
    return _silu(d * lax.rsqrt(var + EPS) * lng + lnb)


def _conv_prompt_kernel(v_ref, h_ref, wdw_ref, bdw_ref, lng_ref, lnb_ref, w2_ref, b2_ref,
                        o_ref, xs, ext, wb, cbuf, *, tl):
    l = pl.program_id(1)
    nb = CONV_WIDTH - 1

    @pl.when((pl.program_id(0) == 0) & (l == 0))
    def _():
        for k in range(CONV_WIDTH):
            wb[k * SUBLANES:(k + 1) * SUBLANES, :] = jnp.broadcast_to(
                wdw_ref[k:k + 1, :], (SUBLANES, D_MODEL))

    @pl.when(l == 0)
    def _():
        xs[0:PERM_ROWS, :] = jnp.zeros((PERM_ROWS, D_MODEL), F32)

    @pl.when(l > 0)
    def _():
        xs[0:PERM_ROWS, :] = xs[tl:tl + PERM_ROWS, :]

    xs[PERM_ROWS:PERM_ROWS + tl, :] = v_ref[...]
    sub = lax.broadcasted_iota(jnp.int32, (SUBLANES, D_MODEL), 0)
    lw = 2 * LANES

    def block(c, carry):
        r0 = pl.multiple_of(c * PERM_ROWS, PERM_ROWS)
        for m in range(1, nb + 1):
            jm = (PERM_VROWS if m <= PERM_VROWS else 2 * PERM_VROWS) - m
            cur = xs[pl.ds(r0 + PERM_ROWS + SUBLANES * jm, SUBLANES), :]
            prv = xs[pl.ds(r0 + SUBLANES * jm, SUBLANES), :]
            ext[(nb - m) * SUBLANES:(nb - m + 1) * SUBLANES, :] = _edge_vrow(cur, prv, m, sub)
        ext[nb * SUBLANES:nb * SUBLANES + PERM_ROWS, :] = xs[pl.ds(r0 + PERM_ROWS, PERM_ROWS), :]
        for lb in range(0, D_MODEL, lw):
            accs = [jnp.broadcast_to(bdw_ref[:, lb:lb + lw], (SUBLANES, lw))] * PERM_VROWS
            for k in range(CONV_WIDTH):
                w8 = wb[k * SUBLANES:(k + 1) * SUBLANES, lb:lb + lw]
                accs = [a + ext[(j + k) * SUBLANES:(j + k + 1) * SUBLANES, lb:lb + lw] * w8
                        for j, a in enumerate(accs)]
            for j, a in enumerate(accs):
                cbuf[pl.ds(r0 + j * SUBLANES, SUBLANES), lb:lb + lw] = a
        return carry

    lax.fori_loop(0, tl // PERM_ROWS, block, 0)
    y = _ln_silu(cbuf[...], lng_ref[...], lnb_ref[...]).astype(BF16)
    y = _reorder_rows(_unperm_matrix(), y)
    o_ref[...] = h_ref[...] + _dot(y, w2_ref[...]) + b2_ref[...]


def _conv_prompt(v, h, wdw, bdw, lng, lnb, w2, b2, layer, tl):
    bsz, seq, _ = v.shape
    tile = pl.BlockSpec((None, tl, D_MODEL), lambda b, l: (b, l, 0))
    return pl.pallas_call(
        functools.partial(_conv_prompt_kernel, tl=tl),
        grid=(bsz, seq // tl),
        in_specs=[tile, tile] + _conv_weight_specs(layer),
        out_specs=tile,
        out_shape=jax.ShapeDtypeStruct(v.shape, F32),
        scratch_shapes=[pltpu.VMEM((tl + PERM_ROWS, D_MODEL), F32),
                        pltpu.VMEM(((CONV_WIDTH - 1) * SUBLANES + PERM_ROWS, D_MODEL), F32),
                        pltpu.VMEM((CONV_WIDTH * SUBLANES, D_MODEL), F32),
                        pltpu.VMEM((tl, D_MODEL), F32)],
        compiler_params=_cparams(2),
        name="conv_prompt",
    )(v, h, wdw, bdw, lng, lnb, w2, b2)


def _stack_plan(layer, n_layers, nb, fill):
    park = 0 if layer > 0 else nb - 1
    if fill:
        return (n_layers, nb), (lambda ls, i: jnp.where(ls == layer, i, park)), (lambda ls: ls)
    return (1, nb), (lambda ls, i: i), (lambda ls: layer)


def _own_slot_or_zero(fill, layer, state_out_ref, body):
    if not fill:
        body()
        return

    @pl.when(pl.program_id(0) != layer)
    def _():
        state_out_ref[...] = jnp.zeros(state_out_ref.shape, F32)

    pl.when(pl.program_id(0) == layer)(body)


def _conv_step_kernel(v_ref, h_ref, st_ref, wdw_ref, bdw_ref, lng_ref, lnb_ref, w2_ref, b2_ref,
                      *rest, layer, fill):
    o_ref, nst_ref = rest[-2:]
    nk = CONV_WIDTH - 1

    def body():
        v = v_ref[...]
        acc = bdw_ref[...] + v * wdw_ref[nk:nk + 1, :]
        for k in range(nk):
            acc = acc + st_ref[k] * wdw_ref[k:k + 1, :]
        nst_ref[0:nk - 1] = st_ref[1:nk]
        nst_ref[nk - 1] = v
        y = _ln_silu(acc, lng_ref[...], lnb_ref[...]).astype(BF16)
        o_ref[...] = h_ref[...] + _dot(y, w2_ref[...]) + b2_ref[...]

    _own_slot_or_zero(fill, layer, nst_ref, body)


def _conv_step(v, h, states, wdw, bdw, lng, lnb, w2, b2, layer, tb, prev_out):
    n_layers, _, bsz, _ = states.shape
    fill = prev_out is None
    grid, bi, li = _stack_plan(layer, n_layers, bsz // tb, fill)
    row = pl.BlockSpec((tb, D_MODEL), lambda ls, i: (bi(ls, i), 0))
    sblk = (None, CONV_WIDTH - 1, tb, D_MODEL)
    in_specs = [row, row, pl.BlockSpec(sblk, lambda ls, i: (layer, 0, bi(ls, i), 0))]
    in_specs += _conv_weight_specs(layer)
    args = [v, h, states, wdw, bdw, lng, lnb, w2, b2]
    aliases = {}
    if not fill:
        in_specs.append(pl.BlockSpec(memory_space=pl.ANY))
        args.append(prev_out)
        aliases = {len(args) - 1: 1}
    return pl.pallas_call(
        functools.partial(_conv_step_kernel, layer=layer, fill=fill),
        grid=grid,
        in_specs=in_specs,
        out_specs=[row, pl.BlockSpec(sblk, lambda ls, i: (li(ls), 0, i, 0))],
        out_shape=[jax.ShapeDtypeStruct((bsz, D_MODEL), F32),
                   jax.ShapeDtypeStruct(states.shape, F32)],
        input_output_aliases=aliases,
        compiler_params=_cparams(2),
        name="conv_step",
    )(*args)


def _ffn_ple_kernel(h_ref, p_ref, gf_ref, wg_ref, wu_ref, wd_ref, gp_ref, wpg_ref, wpp_ref,
                    gfin_ref, o_ref, *, final):
    h = h_ref[...]
    u = _rms(h, gf_ref[...]).astype(BF16)
    acc = h
    for f in range(0, D_FF, FF_CHUNK):
        gate = _dot(u, wg_ref[:, f:f + FF_CHUNK])
        up = _dot(u, wu_ref[:, f:f + FF_CHUNK])
        a = (_silu(gate) * up).astype(BF16)
        acc = acc + _dot(a, wd_ref[f:f + FF_CHUNK, :])
    pg = _sigmoid(_dot(_rms(acc, gp_ref[...]).astype(BF16), wpg_ref[...]))
    out = acc + _dot(p_ref[...].astype(BF16), wpp_ref[...]) * pg
    if final:
        out = _rms(out, gfin_ref[...])
    o_ref[...] = out


def _ffn_ple(h, p, gf, wg, wu, wd, gp, wpg, wpp, gfin, layer, tm, final):
    m = h.shape[0]
    vec = _layer((1, D_MODEL), layer)
    return pl.pallas_call(
        functools.partial(_ffn_ple_kernel, final=final),
        grid=(m // tm,),
        in_specs=[pl.BlockSpec((tm, D_MODEL), lambda i: (i, 0)),
                  pl.BlockSpec((None, tm, PLE_DIM), lambda i: (layer, i, 0)),
                  vec, _layer((D_MODEL, D_FF), layer), _layer((D_MODEL, D_FF), layer),
                  _layer((D_FF, D_MODEL), layer), vec, _layer((D_MODEL, D_MODEL), layer),
                  _layer((PLE_DIM, D_MODEL), layer), _full((1, D_MODEL))],
        out_specs=pl.BlockSpec((tm, D_MODEL), lambda i: (i, 0)),
        out_shape=jax.ShapeDtypeStruct((m, D_MODEL), F32),
        compiler_params=_cparams(1),
        name="ffn_ple",
    )(h, p, gf, wg, wu, wd, gp, wpg, wpp, gfin)


def _gated_norm(y, gate, ng):
    yg = y * gate
    parts = []
    for g in range(N_GROUPS):
        blk = yg[:, g * GROUP_W:(g + 1) * GROUP_W]
        ms = jnp.mean(blk * blk, axis=-1, keepdims=True)
        parts.append(blk * lax.rsqrt(ms + EPS))
    return (jnp.concatenate(parts, axis=-1) * ng).astype(BF16)


def _lane_block_ids(shape, width):
    return lax.broadcasted_iota(jnp.int32, shape, 1) // width


def _expand_heads(v, g, rows):
    blk = _lane_block_ids((rows, GROUP_W), HEAD_DIM)
    out = jnp.zeros((rows, GROUP_W), F32)
    for r in range(HEADS_PER_GROUP):
        hh = g * HEADS_PER_GROUP + r
        col = jnp.broadcast_to(v[:, hh:hh + 1], (rows, GROUP_W))
        out = jnp.where(blk == r, col, out)
    return out


def _mamba_prompt_kernel(hn_ref, hc_ref, g_ref, wzx_ref, wdt_ref, dtb_ref, alog_ref, wcv_ref,
                         bcv_ref, dsk_ref, ng_ref, wout_ref,
                         o_ref, st_ref, cs_ref,
                         z_scr, xb_scr, dt_scr, u_scr, gz_scr, dtc_scr, x_scr, b_scr, c_scr, y_scr,
                         s_scr, *, tl, nt):
    s = pl.program_id(0)
    q = SSD_CHUNK
    nk = SSM_CONV_WIDTH - 1
    first_in_seq = lax.rem(jnp.maximum(s - 1, 0), nt) == 0
    next_starts_seq = lax.rem(s, nt) == 0

    @pl.when(s == 0)
    def _():
        z_scr[...] = jnp.zeros(z_scr.shape, F32)
        xb_scr[...] = jnp.zeros(xb_scr.shape, F32)
        dt_scr[...] = jnp.zeros(dt_scr.shape, F32)
        s_scr[...] = jnp.zeros(s_scr.shape, F32)

    gz_scr[...] = _silu(z_scr[...])
    dtc_scr[...] = dt_scr[...]
    for k in range(nk):
        r = tl + SSM_EDGE - (nk - k) * SUBLANES + SUBLANES - 1
        cs_ref[k:k + 1, :] = xb_scr[r:r + 1, :]

    u_scr[...] = _reorder_rows(_perm_matrix(), _rms(hn_ref[...], g_ref[...]).astype(BF16))
    dt_scr[...] = _softplus(_dot(u_scr[...], wdt_ref[...]) + dtb_ref[...])
    slab_w = 2 * LANES
    slabs = list(range(0, D_INNER + SSM_CONV_DIM, slab_w))

    def emit_in_proj_slab():
        if not slabs:
            return
        c0 = slabs.pop(0)
        res = _dot(u_scr[...], wzx_ref[:, c0:c0 + slab_w])
        if c0 < D_INNER:
            z_scr[:, c0:c0 + slab_w] = res
        else:
            xc = slice(c0 - D_INNER, c0 - D_INNER + slab_w)
            edge = jnp.where(next_starts_seq, 0.0, xb_scr[tl:tl + SSM_EDGE, xc])
            xb_scr[SSM_EDGE:SSM_EDGE + tl, xc] = res
            xb_scr[0:SSM_EDGE, xc] = edge

    cw = 4 * LANES
    sub = lax.broadcasted_iota(jnp.int32, (SUBLANES, cw), 0)
    for cb0 in range(0, SSM_CONV_DIM, cw):
        cl = slice(cb0, cb0 + cw)
        for c in range(tl // q):
            base = c * q
            cur = xb_scr[base + SSM_EDGE:base + SSM_EDGE + q, cl]
            edges = []
            for m in range(nk, 0, -1):
                jm = PERM_VROWS - m
                cur_v = cur[jm * SUBLANES:(jm + 1) * SUBLANES, :]
                prv_v = xb_scr[base + (nk - m) * SUBLANES:base + (nk - m + 1) * SUBLANES, cl]
                edges.append(_edge_vrow(cur_v, prv_v, m, sub))
            hist = jnp.concatenate(edges + [cur], axis=0)
            conv = bcv_ref[:, cl] + cur * wcv_ref[nk:nk + 1, cl]
            for s in range(1, SSM_CONV_WIDTH):
                lo = SSM_EDGE - s * SUBLANES
                conv = conv + hist[lo:lo + q, :] * wcv_ref[nk - s:nk - s + 1, cl]
            act = _silu(conv)
            rows = slice(base, base + q)
            if cb0 < D_INNER:
                x_scr[rows, cl] = act
            elif cb0 < D_INNER + N_GROUPS * D_STATE:
                b_scr[rows, cb0 - D_INNER:cb0 - D_INNER + cw] = act.astype(BF16)
            else:
                c0 = cb0 - D_INNER - N_GROUPS * D_STATE
                c_scr[rows, c0:c0 + cw] = act.astype(BF16)
        if slabs[0] < D_INNER:
            emit_in_proj_slab()

    a_neg = -jnp.exp(alog_ref[...])
    causal = _time_of_row((q, q), 0) >= _time_of_row((q, q), 1)
    tril = causal.astype(BF16)
    low_half = lax.broadcasted_iota(jnp.int32, (q, LANES), 1) < HEAD_DIM
    blk = _lane_block_ids((q, GROUP_W), HEAD_DIM)

    def pair_expand(cols):
        return jnp.concatenate([jnp.where(low_half, cols[0], cols[1]),
                                jnp.where(low_half, cols[2], cols[3])], axis=-1)

    for c in range(tl // q):
        rows = slice(c * q, (c + 1) * q)
        dtc = dtc_scr[rows, :]
        a = dtc * a_neg
        a_hi, a_mid, a_lo = _split3(a)
        a_cum = _dot(tril, a_hi) + _dot(tril, a_mid) + _dot(tril, a_lo)
        a_cum_t = a_cum.T
        for g in range(N_GROUPS):
            gs = slice(g * GROUP_W, (g + 1) * GROUP_W)
            ns = slice(g * D_STATE, (g + 1) * D_STATE)
            heads = [g * HEADS_PER_GROUP + r for r in range(HEADS_PER_GROUP)]
            bg = b_scr[rows, ns]
            cg = c_scr[rows, ns]
            xg = x_scr[rows, gs]
            sg = s_scr[gs, :]
            if c == 0:
                sg = jnp.where(first_in_seq, 0.0, sg)
            cb = _dot_nt(cg, bg)
            y_off = _dot_nt(cg, sg.astype(BF16))
            bc_a = [jnp.broadcast_to(a_cum[:, hh:hh + 1], (q, LANES)) for hh in heads]
            bc_dt = [jnp.broadcast_to(dtc[:, hh:hh + 1], (q, LANES)) for hh in heads]
            a_e = pair_expand(bc_a)
            xdt = xg * pair_expand(bc_dt)
            ms = []
            for r, hh in enumerate(heads):
                seg = bc_a[r] - a_cum_t[hh:hh + 1, :]
                lm = jnp.exp(jnp.where(causal, seg, -jnp.inf))
                ms.append((cb * lm).astype(BF16))
            xblk = jnp.concatenate(
                [jnp.where(blk == r, xdt, 0.0).astype(BF16) for r in range(HEADS_PER_GROUP)], axis=0)
            y = (_dot(jnp.concatenate(ms, axis=-1), xblk) + y_off * jnp.exp(a_e)
                 + xg * dsk_ref[:, gs])
            y_scr[rows, gs] = y
            xs = (xdt * jnp.exp(a_e[q - 1:q, :] - a_e)).astype(BF16)
            add = _dot_tn(xs, bg)
            for r, hh in enumerate(heads):
                cd = jnp.exp(bc_a[r][q - 1:q, :])
                hs = slice(hh * HEAD_DIM, (hh + 1) * HEAD_DIM)
                ps = slice(r * HEAD_DIM, (r + 1) * HEAD_DIM)
                s_scr[hs, :] = cd * sg[ps, :] + add[ps, :]
            emit_in_proj_slab()

    while slabs:
        emit_in_proj_slab()
    st_ref[...] = s_scr[...]
    yn = _reorder_rows(_unperm_matrix(), _gated_norm(y_scr[...], gz_scr[...], ng_ref[...]))
    o_ref[...] = hc_ref[...] + _dot(yn, wout_ref[...])


def _mamba_weight_specs(mix_layer, layer):
    return [_layer((1, D_MODEL), mix_layer),
            pl.BlockSpec((None, D_MODEL, D_INNER + SSM_CONV_DIM), lambda *_: (layer, 0, 0)),
            _layer((D_MODEL, LANES), layer), _layer((1, LANES), layer), _layer((1, LANES), layer),
            _layer((SSM_CONV_WIDTH, SSM_CONV_DIM), layer), _layer((1, SSM_CONV_DIM), layer)]


def _mamba_prompt(h, g, wzx, wdt, dtb, alog, wcv, bcv, dsk, ng, wout, mix_layer, layer, tl):
    bsz, seq, _ = h.shape
    nt = seq // tl
    n_tiles = bsz * nt
    nconv = SSM_CONV_WIDTH - 1

    def cur(s):
        return jnp.maximum(s - 1, 0)

    def nxt(s):
        return jnp.minimum(s, n_tiles - 1)

    tile_cur = pl.BlockSpec((None, tl, D_MODEL), lambda s: (cur(s) // nt, cur(s) % nt, 0))
    tile_nxt = pl.BlockSpec((None, tl, D_MODEL), lambda s: (nxt(s) // nt, nxt(s) % nt, 0))
    return pl.pallas_call(
        functools.partial(_mamba_prompt_kernel, tl=tl, nt=nt),
        grid=(n_tiles + 1,),
        in_specs=[tile_nxt, tile_cur] + _mamba_weight_specs(mix_layer, layer) + [
            _layer((1, D_INNER), layer), _layer((1, D_INNER), layer),
            _layer((D_INNER, D_MODEL), layer)],
        out_specs=[tile_cur,
                   pl.BlockSpec((None, D_INNER, D_STATE), lambda s: (cur(s) // nt, 0, 0)),
                   pl.BlockSpec((None, nconv, SSM_CONV_DIM), lambda s: (cur(s) // nt, 0, 0))],
        out_shape=[jax.ShapeDtypeStruct(h.shape, F32),
                   jax.ShapeDtypeStruct((bsz, D_INNER, D_STATE), F32),
                   jax.ShapeDtypeStruct((bsz, nconv, SSM_CONV_DIM), F32)],
        scratch_shapes=[pltpu.VMEM((tl, D_INNER), F32),
                        pltpu.VMEM((tl + SSM_EDGE, SSM_CONV_DIM), F32),
                        pltpu.VMEM((tl, LANES), F32),
                        pltpu.VMEM((tl, D_MODEL), BF16),
                        pltpu.VMEM((tl, D_INNER), F32),
                        pltpu.VMEM((tl, LANES), F32),
                        pltpu.VMEM((tl, D_INNER), F32),
                        pltpu.VMEM((tl, N_GROUPS * D_STATE), BF16),
                        pltpu.VMEM((tl, N_GROUPS * D_STATE), BF16),
                        pltpu.VMEM((tl, D_INNER), F32),
                        pltpu.VMEM((D_INNER, D_STATE), F32)],
        compiler_params=_cparams(1),
        name="mamba_prompt",
    )(h, h, g, wzx, wdt, dtb, alog, wcv, bcv, dsk, ng, wout)


def _mamba_step_in_kernel(h_ref, g_ref, wzx_ref, wdt_ref, dtb_ref, alog_ref, wcv_ref, bcv_ref,
                          cs_ref, *rest, layer, fill):
    z_ref, act_ref, dt_ref, da_ref, ncs_ref = rest[-5:]
    nk = SSM_CONV_WIDTH - 1

    def body():
        u = _rms(h_ref[...], g_ref[...]).astype(BF16)
        z_ref[...] = _dot(u, wzx_ref[:, :D_INNER])
        xb = _dot(u, wzx_ref[:, D_INNER:])
        dt = _softplus(_dot(u, wdt_ref[...]) + dtb_ref[...])
        dt_ref[...] = dt
        da_ref[...] = jnp.exp(dt * (-jnp.exp(alog_ref[...])))
        conv = bcv_ref[...] + xb * wcv_ref[nk:nk + 1, :]
        for k in range(nk):
            conv = conv + cs_ref[k] * wcv_ref[k:k + 1, :]
        act_ref[...] = _silu(conv)
        ncs_ref[0:nk - 1] = cs_ref[1:nk]
        ncs_ref[nk - 1] = xb

    _own_slot_or_zero(fill, layer, ncs_ref, body)


def _mamba_step_in(h, g, wzx, wdt, dtb, alog, wcv, bcv, states, mix_layer, layer, prev_out):
    n_layers, nk, bsz, _ = states.shape
    fill = prev_out is None
    grid, _, li = _stack_plan(layer, n_layers, 1, fill)
    sblk = (None, nk, bsz, SSM_CONV_DIM)
    shapes = [(bsz, D_INNER), (bsz, SSM_CONV_DIM), (bsz, LANES), (bsz, LANES)]
    in_specs = [_full((bsz, D_MODEL))] + _mamba_weight_specs(mix_layer, layer)
    in_specs.append(pl.BlockSpec(sblk, lambda ls, i: (layer, 0, 0, 0)))
    args = [h, g, wzx, wdt, dtb, alog, wcv, bcv, states]
    aliases = {}
    if not fill:
        in_specs.append(pl.BlockSpec(memory_space=pl.ANY))
        args.append(prev_out)
        aliases = {len(args) - 1: len(shapes)}
    return pl.pallas_call(
        functools.partial(_mamba_step_in_kernel, layer=layer, fill=fill),
        grid=grid,
        in_specs=in_specs,
        out_specs=[_full(s) for s in shapes] + [pl.BlockSpec(sblk, lambda ls, i: (li(ls), 0, 0, 0))],
        out_shape=[jax.ShapeDtypeStruct(s, F32) for s in shapes]
        + [jax.ShapeDtypeStruct(states.shape, F32)],
        input_output_aliases=aliases,
        compiler_params=_cparams(2),
        name="mamba_step_in",
    )(*args)


def _ssm_step_kernel(da_ref, s_ref, act_ref, dt_ref, dsk_ref, *rest, tb, layer, fill):
    ns_ref, y_ref = rest[-2:]
    _own_slot_or_zero(
        fill, layer, ns_ref,
        lambda: _ssm_step_body(da_ref, s_ref, act_ref, dt_ref, dsk_ref, ns_ref, y_ref, tb))


def _ssm_step_body(da_ref, s_ref, act_ref, dt_ref, dsk_ref, ns_ref, y_ref, tb):
    i = pl.program_id(1)
    rid = lax.broadcasted_iota(jnp.int32, (tb, 1), 0)
    zpad = jnp.zeros((tb, GROUP_W), F32)
    dt = dt_ref[...]
    for g in range(N_GROUPS):
        gs = slice(g * GROUP_W, (g + 1) * GROUP_W)
        xg = act_ref[:, gs]
        bg = act_ref[:, D_INNER + g * D_STATE:D_INNER + (g + 1) * D_STATE]
        cg = act_ref[:, D_INNER + (N_GROUPS + g) * D_STATE:D_INNER + (N_GROUPS + g + 1) * D_STATE]
        xdt = xg * _expand_heads(dt, g, tb)
        y8 = jnp.zeros((tb, GROUP_W), F32)
        for r in range(tb):
            sel = rid == r
            x1 = jnp.concatenate([jnp.where(sel, xdt, 0.0), zpad], axis=0).astype(BF16)
            b1 = jnp.concatenate([jnp.where(sel, bg, 0.0), zpad[:, :D_STATE]], axis=0).astype(BF16)
            c1 = jnp.concatenate([jnp.where(sel, cg, 0.0), zpad[:, :D_STATE]], axis=0).astype(BF16)
            add = _dot_tn(x1, b1)
            news = []
            for k in range(HEADS_PER_GROUP):
                hh = g * HEADS_PER_GROUP + k
                da = da_ref[i * tb + r, hh]
                sn = da * s_ref[r, hh] + add[k * HEAD_DIM:(k + 1) * HEAD_DIM, :]
                ns_ref[r, hh] = sn
                news.append(sn)
            sg = jnp.concatenate(news, axis=0).astype(BF16)
            y8 = y8 + _dot_nt(c1, sg)[:tb, :]
        y_ref[:, gs] = y8 + xg * dsk_ref[:, gs]


def _ssm_step(da, states, layer, act, dt, dsk, tb, prev_out):
    n_layers, bsz = states.shape[:2]
    fill = prev_out is None
    grid, bi, li = _stack_plan(layer, n_layers, bsz // tb, fill)
    sblk = (None, tb, N_HEADS, HEAD_DIM, D_STATE)
    in_specs = [pl.BlockSpec(memory_space=pltpu.SMEM),
                pl.BlockSpec(sblk, lambda ls, i: (layer, bi(ls, i), 0, 0, 0)),
                pl.BlockSpec((tb, SSM_CONV_DIM), lambda ls, i: (bi(ls, i), 0)),
                pl.BlockSpec((tb, LANES), lambda ls, i: (bi(ls, i), 0)),
                _layer((1, D_INNER), layer)]
    args = [da, states, act, dt, dsk]
    aliases = {}
    if not fill:
        in_specs.append(pl.BlockSpec(memory_space=pl.ANY))
        args.append(prev_out)
        aliases = {len(args) - 1: 0}
    return pl.pallas_call(
        functools.partial(_ssm_step_kernel, tb=tb, layer=layer, fill=fill),
        grid=grid,
        in_specs=in_specs,
        out_specs=[pl.BlockSpec(sblk, lambda ls, i: (li(ls), i, 0, 0, 0)),
                   pl.BlockSpec((tb, D_INNER), lambda ls, i: (bi(ls, i), 0))],
        out_shape=[jax.ShapeDtypeStruct(states.shape, F32),
                   jax.ShapeDtypeStruct((bsz, D_INNER), F32)],
        input_output_aliases=aliases,
        compiler_params=_cparams(2),
        name="ssm_step",
    )(*args)


def _gnorm_out_kernel(y_ref, z_ref, h_ref, ng_ref, wout_ref, o_ref):
    yn = _gated_norm(y_ref[...], _silu(z_ref[...]), ng_ref[...])
    o_ref[...] = h_ref[...] + _dot(yn, wout_ref[...])


def _gnorm_out(y, z, h, ng, wout, layer):
    bsz = y.shape[0]
    return pl.pallas_call(
        _gnorm_out_kernel,
        grid=(1,),
        in_specs=[_full((bsz, D_INNER)), _full((bsz, D_INNER)), _full((bsz, D_MODEL)),
                  _layer((1, D_INNER), layer), _layer((D_INNER, D_MODEL), layer)],
        out_specs=_full((bsz, D_MODEL)),
        out_shape=jax.ShapeDtypeStruct((bsz, D_MODEL), F32),
        compiler_params=_cparams(1),
        name="gnorm_out",
    )(y, z, h, ng, wout)


def _rows(x):
    return x.reshape(x.shape[0], 1, x.shape[1])


def _pad_lanes(x):
    return jnp.pad(x, [(0, 0)] * (x.ndim - 1) + [(0, LANES - x.shape[-1])])


def kernel(x_prompt, x_sample, p_prompt, p_sample, state_conv_mixer, state_ssm, state_ssm_conv, g_mix, g_ffn, g_ple, g_final, cv_w_pw1, cv_b_pw1, cv_w_dw, cv_b_dw, cv_ln_g, cv_ln_b, cv_w_pw2, cv_b_pw2, ssm_w_in, ssm_w_conv, ssm_b_conv, ssm_dt_bias, ssm_a_log, ssm_d, ssm_norm_g, ssm_w_out, ffn_w_gate, ffn_w_up, ffn_w_down, ple_w_proj, ple_w_gate):
    depth = g_mix.shape[0]
    bp, seq, _ = x_prompt.shape
    bs = x_sample.shape[0]
    mp = bp * seq
    tm_prompt = 512
    tm_ffn = 1024
    tl_conv = 512
    tl_ssm = 256
    zx_w = D_INNER + SSM_CONV_DIM

    gm, gf, gp = _rows(g_mix), _rows(g_ffn), _rows(g_ple)
    gfin = g_final.reshape(1, D_MODEL)
    w1, b1 = cv_w_pw1.astype(BF16), _rows(cv_b_pw1)
    cargs = (cv_w_dw, _rows(cv_b_dw), _rows(cv_ln_g), _rows(cv_ln_b),
             cv_w_pw2.astype(BF16), _rows(cv_b_pw2))
    w_in = ssm_w_in.astype(BF16)
    margs = (w_in, _pad_lanes(ssm_w_in[:, :, zx_w:]).astype(BF16), _rows(_pad_lanes(ssm_dt_bias)),
             _rows(_pad_lanes(ssm_a_log)), ssm_w_conv, _rows(ssm_b_conv))
    dsk = _rows(jnp.repeat(ssm_d, HEAD_DIM, axis=1))
    ng = _rows(ssm_norm_g)
    wout = ssm_w_out.astype(BF16)
    fargs = (gf, ffn_w_gate.astype(BF16), ffn_w_up.astype(BF16), ffn_w_down.astype(BF16), gp,
             ple_w_gate.astype(BF16), ple_w_proj.astype(BF16), gfin)
    pp = p_prompt.reshape(depth, mp, PLE_DIM)
    ps = p_sample.reshape(depth, bs, PLE_DIM)
    conv_st = jnp.swapaxes(state_conv_mixer, 1, 2)
    ssmc_st = jnp.swapaxes(state_ssm_conv, 1, 2)

    hp = x_prompt
    hs = x_sample.reshape(bs, D_MODEL)
    conv_p, ssm_p, ssmc_p = [], [], []
    conv_s = ssm_s = ssmc_s = None

    for i in range(depth):
        j = i // 2
        if i % 2 == 0:
            vp = _pw1_glu(hp.reshape(mp, D_MODEL), gm, w1, b1, i, j, tm_prompt, True)
            vp = vp.reshape(bp, seq, D_MODEL)
            tail = vp[:, seq - PERM_ROWS:, :].reshape(bp, PERM_VROWS, SUBLANES, D_MODEL)
            tail = jnp.swapaxes(tail, 1, 2).reshape(bp, PERM_ROWS, D_MODEL)
            conv_p.append(tail[:, PERM_ROWS - (CONV_WIDTH - 1):, :])
            hp = _conv_prompt(vp, hp, *cargs, j, tl_conv)
            vs = _pw1_glu(hs, gm, w1, b1, i, j, bs, False)
            hs, conv_s = _conv_step(vs, hs, conv_st, *cargs, j, 32, conv_s)
        else:
            hp, fin, ncs = _mamba_prompt(hp, gm, *margs, dsk, ng, wout, i, j, tl_ssm)
            ssm_p.append(fin.reshape(bp, N_HEADS, HEAD_DIM, D_STATE))
            ssmc_p.append(ncs)
            z, act, dt, da, ssmc_s = _mamba_step_in(hs, gm, *margs, ssmc_st, i, j, ssmc_s)
            ssm_s, y = _ssm_step(da[:, :N_HEADS], state_ssm, j, act, dt, dsk, 8, ssm_s)
            hs = _gnorm_out(y, z, hs, ng, wout, j)
        final = i == depth - 1
        hp = _ffn_ple(hp.reshape(mp, D_MODEL), pp, *fargs, i, tm_ffn, final)
        hp = hp.reshape(bp, seq, D_MODEL)
        hs = _ffn_ple(hs, ps, *fargs, i, bs, final)

    return (hp, hs.reshape(bs, 1, D_MODEL),
            jnp.stack(conv_p), jnp.stack(ssm_p), jnp.stack(ssmc_p),
            jnp.swapaxes(conv_s, 1, 2), ssm_s, jnp.swapaxes(ssmc_s, 1, 2))
```

```python
import functools

import jax
import jax.numpy as jnp
from jax import lax
from jax.experimental import pallas as pl
from jax.experimental.pallas import tpu as pltpu

F32 = jnp.float32
BF16 = jnp.bfloat16

D_MODEL = 1024
CONV_WIDTH = 31
D_INNER = 2048
HEAD_DIM = 64
N_HEADS = 32
N_GROUPS = 8
HEADS_PER_GROUP = 4
D_STATE = 128
SSM_CONV_WIDTH = 4
SSM_CONV_DIM = 4096
D_FF = 2816
PLE_DIM = 256
EPS = 1e-6

LANES = 128
SUBLANES = 8
GROUP_W = HEADS_PER_GROUP * HEAD_DIM
FF_CHUNK = 256
GLU_SLAB = 256
SSD_CHUNK = 128
PERM_ROWS = SSD_CHUNK
PERM_VROWS = PERM_ROWS // SUBLANES
SSM_EDGE = (SSM_CONV_WIDTH - 1) * SUBLANES
VMEM_LIMIT = 60 * 1024 * 1024


def _cparams(n_axes):
    return pltpu.CompilerParams(
        dimension_semantics=("arbitrary",) * n_axes, vmem_limit_bytes=VMEM_LIMIT)


def _sigmoid(x):
    return 1.0 / (1.0 + jnp.exp(-x))


def _silu(x):
    return x * _sigmoid(x)


def _softplus(x):
    return jnp.maximum(x, 0.0) + jnp.log(1.0 + jnp.exp(-jnp.abs(x)))


def _rms(x, g):
    return x * lax.rsqrt(jnp.mean(x * x, axis=-1, keepdims=True) + EPS) * g


def _dot(a, b):
    return jnp.dot(a, b, preferred_element_type=F32)


def _dot_nt(a, b):
    return lax.dot_general(a, b, (((1,), (1,)), ((), ())), preferred_element_type=F32)


def _dot_tn(a, b):
    return lax.dot_general(a, b, (((0,), (0,)), ((), ())), preferred_element_type=F32)


def _split3(x):
    hi = x.astype(BF16)
    r1 = x - hi.astype(F32)
    mid = r1.astype(BF16)
    lo = (r1 - mid.astype(F32)).astype(BF16)
    return hi, mid, lo


def _full(shape):
    return pl.BlockSpec(shape, lambda *_: (0,) * len(shape))


def _layer(shape, layer):
    return pl.BlockSpec((None,) + tuple(shape), lambda *_: (layer,) + (0,) * len(shape))


def _time_of_row(shape, axis):
    r = lax.broadcasted_iota(jnp.int32, shape, axis)
    return PERM_VROWS * (r & (SUBLANES - 1)) + lax.shift_right_logical(r, 3)


def _perm_matrix():
    col = lax.broadcasted_iota(jnp.int32, (PERM_ROWS, PERM_ROWS), 1)
    return (col == _time_of_row((PERM_ROWS, PERM_ROWS), 0)).astype(BF16)


def _unperm_matrix():
    row = lax.broadcasted_iota(jnp.int32, (PERM_ROWS, PERM_ROWS), 0)
    return (row == _time_of_row((PERM_ROWS, PERM_ROWS), 1)).astype(BF16)


def _reorder_rows(mat, x):
    blocks = [_dot(mat, x[c:c + PERM_ROWS, :]).astype(BF16) for c in range(0, x.shape[0], PERM_ROWS)]
    return blocks[0] if len(blocks) == 1 else jnp.concatenate(blocks, axis=0)


def _edge_vrow(cur, prv, m, sub):
    s = 1 if m <= PERM_VROWS else 2
    return jnp.where(sub < s, pltpu.roll(prv, s, 0), pltpu.roll(cur, s, 0))


def _pw1_glu_kernel(h_ref, g_ref, w_ref, b_ref, v_ref, *, permute):
    u = _rms(h_ref[...], g_ref[...]).astype(BF16)
    if permute:
        u = _reorder_rows(_perm_matrix(), u)
    for c0 in range(0, D_MODEL, GLU_SLAB):
        lin = _dot(u, w_ref[:, c0:c0 + GLU_SLAB]) + b_ref[:, c0:c0 + GLU_SLAB]
        gate = (_dot(u, w_ref[:, D_MODEL + c0:D_MODEL + c0 + GLU_SLAB])
                + b_ref[:, D_MODEL + c0:D_MODEL + c0 + GLU_SLAB])
        v_ref[:, c0:c0 + GLU_SLAB] = lin * _sigmoid(gate)


def _pw1_glu(h, g, w, b, mix_layer, conv_layer, tm, permute):
    m = h.shape[0]
    return pl.pallas_call(
        functools.partial(_pw1_glu_kernel, permute=permute),
        grid=(m // tm,),
        in_specs=[pl.BlockSpec((tm, D_MODEL), lambda i: (i, 0)),
                  _layer((1, D_MODEL), mix_layer), _layer((D_MODEL, 2 * D_MODEL), conv_layer),
                  _layer((1, 2 * D_MODEL), conv_layer)],
        out_specs=pl.BlockSpec((tm, D_MODEL), lambda i: (i, 0)),
        out_shape=jax.ShapeDtypeStruct((m, D_MODEL), F32),
        compiler_params=_cparams(1),
        name="pw1_glu",
    )(h, g, w, b)


def _conv_weight_specs(layer):
    vec = _layer((1, D_MODEL), layer)
    return [_layer((CONV_WIDTH, D_MODEL), layer), vec, vec, vec,
            _layer((D_MODEL, D_MODEL), layer), vec]


def _ln_silu(c, lng, lnb):
    mu = jnp.mean(c, axis=-1, keepdims=True)
    d = c - mu
    var = jnp.mean(d * d, axis=-1, keepdims=True)
    return _silu(d * lax.rsqrt(var + EPS) * lng + lnb)


def _conv_prompt_kernel(v_ref, h_ref, wdw_ref, bdw_ref, lng_ref, lnb_ref, w2_ref, b2_ref,
                        o_ref, xs, ext, wb, cbuf, *, tl):
    l = pl.program_id(1)
    nb = CONV_WIDTH - 1

    @pl.when((pl.program_id(0) == 0) & (l == 0))
    def _():
        for k in range(CONV_WIDTH):
            wb[k * SUBLANES:(k + 1) * SUBLANES, :] = jnp.broadcast_to(
                wdw_ref[k:k + 1, :], (SUBLANES, D_MODEL))

    @pl.when(l == 0)
    def _():
        xs[0:PERM_ROWS, :] = jnp.zeros((PERM_ROWS, D_MODEL), F32)

    @pl.when(l > 0)
    def _():
        xs[0:PERM_ROWS, :] = xs[tl:tl + PERM_ROWS, :]

    xs[PERM_ROWS:PERM_ROWS + tl, :] = v_ref[...]
    sub = lax.broadcasted_iota(jnp.int32, (SUBLANES, D_MODEL), 0)
    lw = 2 * LANES

    def block(c, carry):
        r0 = pl.multiple_of(c * PERM_ROWS, PERM_ROWS)
        for m in range(1, nb + 1):
            jm = (PERM_VROWS if m <= PERM_VROWS else 2 * PERM_VROWS) - m
            cur = xs[pl.ds(r0 + PERM_ROWS + SUBLANES * jm, SUBLANES), :]
            prv = xs[pl.ds(r0 + SUBLANES * jm, SUBLANES), :]
            ext[(nb - m) * SUBLANES:(nb - m + 1) * SUBLANES, :] = _edge_vrow(cur, prv, m, sub)
        ext[nb * SUBLANES:nb * SUBLANES + PERM_ROWS, :] = xs[pl.ds(r0 + PERM_ROWS, PERM_ROWS), :]
        for lb in range(0, D_MODEL, lw):
            accs = [jnp.broadcast_to(bdw_ref[:, lb:lb + lw], (SUBLANES, lw))] * PERM_VROWS
            for k in range(CONV_WIDTH):
                w8 = wb[k * SUBLANES:(k + 1) * SUBLANES, lb:lb + lw]
                accs = [a + ext[(j + k) * SUBLANES:(j + k + 1) * SUBLANES, lb:lb + lw] * w8
                        for j, a in enumerate(accs)]
            for j, a in enumerate(accs):
                cbuf[pl.ds(r0 + j * SUBLANES, SUBLANES), lb:lb + lw] = a
        return carry

    lax.fori_loop(0, tl // PERM_ROWS, block, 0)
    y = _ln_silu(cbuf[...], lng_ref[...], lnb_ref[...]).astype(BF16)
    y = _reorder_rows(_unperm_matrix(), y)
    o_ref[...] = h_ref[...] + _dot(y, w2_ref[...]) + b2_ref[...]


def _conv_prompt(v, h, wdw, bdw, lng, lnb, w2, b2, layer, tl):
    bsz, seq, _ = v.shape
    tile = pl.BlockSpec((None, tl, D_MODEL), lambda b, l: (b, l, 0))
    return pl.pallas_call(
        functools.partial(_conv_prompt_kernel, tl=tl),
        grid=(bsz, seq // tl),
        in_specs=[tile, tile] + _conv_weight_specs(layer),
        out_specs=tile,
        out_shape=jax.ShapeDtypeStruct(v.shape, F32),
        scratch_shapes=[pltpu.VMEM((tl + PERM_ROWS, D_MODEL), F32),
                        pltpu.VMEM(((CONV_WIDTH - 1) * SUBLANES + PERM_ROWS, D_MODEL), F32),
                        pltpu.VMEM((CONV_WIDTH * SUBLANES, D_MODEL), F32),
                        pltpu.VMEM((tl, D_MODEL), F32)],
        compiler_params=_cparams(2),
        name="conv_prompt",
    )(v, h, wdw, bdw, lng, lnb, w2, b2)


def _stack_plan(layer, n_layers, nb, fill):
    park = 0 if layer > 0 else nb - 1
    if fill:
        return (n_layers, nb), (lambda ls, i: jnp.where(ls == layer, i, park)), (lambda ls: ls)
    return (1, nb), (lambda ls, i: i), (lambda ls: layer)


def _own_slot_or_zero(fill, layer, state_out_ref, body):
    if not fill:
        body()
        return

    @pl.when(pl.program_id(0) != layer)
    def _():
        state_out_ref[...] = jnp.zeros(state_out_ref.shape, F32)

    pl.when(pl.program_id(0) == layer)(body)


def _conv_step_kernel(v_ref, h_ref, st_ref, wdw_ref, bdw_ref, lng_ref, lnb_ref, w2_ref, b2_ref,
                      *rest, layer, fill):
    o_ref, nst_ref = rest[-2:]
    nk = CONV_WIDTH - 1

    def body():
        v = v_ref[...]
        acc = bdw_ref[...] + v * wdw_ref[nk:nk + 1, :]
        for k in range(nk):
            acc = acc + st_ref[k] * wdw_ref[k:k + 1, :]
        nst_ref[0:nk - 1] = st_ref[1:nk]
        nst_ref[nk - 1] = v
        y = _ln_silu(acc, lng_ref[...], lnb_ref[...]).astype(BF16)
        o_ref[...] = h_ref[...] + _dot(y, w2_ref[...]) + b2_ref[...]

    _own_slot_or_zero(fill, layer, nst_ref, body)


def _conv_step(v, h, states, wdw, bdw, lng, lnb, w2, b2, layer, tb, prev_out):
    n_layers, _, bsz, _ = states.shape
    fill = prev_out is None
    grid, bi, li = _stack_plan(layer, n_layers, bsz // tb, fill)
    row = pl.BlockSpec((tb, D_MODEL), lambda ls, i: (bi(ls, i), 0))
    sblk = (None, CONV_WIDTH - 1, tb, D_MODEL)
    in_specs = [row, row, pl.BlockSpec(sblk, lambda ls, i: (layer, 0, bi(ls, i), 0))]
    in_specs += _conv_weight_specs(layer)
    args = [v, h, states, wdw, bdw, lng, lnb, w2, b2]
    aliases = {}
    if not fill:
        in_specs.append(pl.BlockSpec(memory_space=pl.ANY))
        args.append(prev_out)
        aliases = {len(args) - 1: 1}
    return pl.pallas_call(
        functools.partial(_conv_step_kernel, layer=layer, fill=fill),
        grid=grid,
        in_specs=in_specs,
        out_specs=[row, pl.BlockSpec(sblk, lambda ls, i: (li(ls), 0, i, 0))],
        out_shape=[jax.ShapeDtypeStruct((bsz, D_MODEL), F32),
                   jax.ShapeDtypeStruct(states.shape, F32)],
        input_output_aliases=aliases,
        compiler_params=_cparams(2),
        name="conv_step",
    )(*args)


def _ffn_ple_kernel(h_ref, p_ref, gf_ref, wg_ref, wu_ref, wd_ref, gp_ref, wpg_ref, wpp_ref,
                    gfin_ref, o_ref, *, final):
    _ffn_ple_body(h_ref[...], p_ref, gf_ref, wg_ref, wu_ref, wd_ref, gp_ref, wpg_ref, wpp_ref,
                  gfin_ref, o_ref, final)


def _ffn_ple_body(h, p_ref, gf_ref, wg_ref, wu_ref, wd_ref, gp_ref, wpg_ref, wpp_ref,
                  gfin_ref, o_ref, final):
    u = _rms(h, gf_ref[...]).astype(BF16)
    acc = h
    for f in range(0, D_FF, FF_CHUNK):
        gate = _dot(u, wg_ref[:, f:f + FF_CHUNK])
        up = _dot(u, wu_ref[:, f:f + FF_CHUNK])
        a = (_silu(gate) * up).astype(BF16)
        acc = acc + _dot(a, wd_ref[f:f + FF_CHUNK, :])
    pg = _sigmoid(_dot(_rms(acc, gp_ref[...]).astype(BF16), wpg_ref[...]))
    out = acc + _dot(p_ref[...].astype(BF16), wpp_ref[...]) * pg
    if final:
        out = _rms(out, gfin_ref[...])
    o_ref[...] = out


def _ffn_ple(h, p, gf, wg, wu, wd, gp, wpg, wpp, gfin, layer, tm, final):
    m = h.shape[0]
    vec = _layer((1, D_MODEL), layer)
    return pl.pallas_call(
        functools.partial(_ffn_ple_kernel, final=final),
        grid=(m // tm,),
        in_specs=[pl.BlockSpec((tm, D_MODEL), lambda i: (i, 0)),
                  pl.BlockSpec((None, tm, PLE_DIM), lambda i: (layer, i, 0)),
                  vec, _layer((D_MODEL, D_FF), layer), _layer((D_MODEL, D_FF), layer),
                  _layer((D_FF, D_MODEL), layer), vec, _layer((D_MODEL, D_MODEL), layer),
                  _layer((PLE_DIM, D_MODEL), layer), _full((1, D_MODEL))],
        out_specs=pl.BlockSpec((tm, D_MODEL), lambda i: (i, 0)),
        out_shape=jax.ShapeDtypeStruct((m, D_MODEL), F32),
        compiler_params=_cparams(1),
        name="ffn_ple",
    )(h, p, gf, wg, wu, wd, gp, wpg, wpp, gfin)


def _gated_norm(y, gate, ng):
    yg = y * gate
    parts = []
    for g in range(N_GROUPS):
        blk = yg[:, g * GROUP_W:(g + 1) * GROUP_W]
        ms = jnp.mean(blk * blk, axis=-1, keepdims=True)
        parts.append(blk * lax.rsqrt(ms + EPS))
    return (jnp.concatenate(parts, axis=-1) * ng).astype(BF16)


def _lane_block_ids(shape, width):
    return lax.broadcasted_iota(jnp.int32, shape, 1) // width


def _expand_heads(v, g, rows):
    blk = _lane_block_ids((rows, GROUP_W), HEAD_DIM)
    out = jnp.zeros((rows, GROUP_W), F32)
    for r in range(HEADS_PER_GROUP):
        hh = g * HEADS_PER_GROUP + r
        col = jnp.broadcast_to(v[:, hh:hh + 1], (rows, GROUP_W))
        out = jnp.where(blk == r, col, out)
    return out


def _mamba_prompt_kernel(hn_ref, hc_ref, g_ref, wzx_ref, wdt_ref, dtb_ref, alog_ref, wcv_ref,
                         bcv_ref, dsk_ref, ng_ref, wout_ref,
                         o_ref, st_ref, cs_ref,
                         z_scr, xb_scr, dt_scr, u_scr, gz_scr, dtc_scr, x_scr, b_scr, c_scr, y_scr,
                         s_scr, *, tl, nt):
    s = pl.program_id(0)
    q = SSD_CHUNK
    nk = SSM_CONV_WIDTH - 1
    first_in_seq = lax.rem(jnp.maximum(s - 1, 0), nt) == 0
    next_starts_seq = lax.rem(s, nt) == 0

    @pl.when(s == 0)
    def _():
        z_scr[...] = jnp.zeros(z_scr.shape, F32)
        xb_scr[...] = jnp.zeros(xb_scr.shape, F32)
        dt_scr[...] = jnp.zeros(dt_scr.shape, F32)
        s_scr[...] = jnp.zeros(s_scr.shape, F32)

    gz_scr[...] = _silu(z_scr[...])
    dtc_scr[...] = dt_scr[...]
    for k in range(nk):
        r = tl + SSM_EDGE - (nk - k) * SUBLANES + SUBLANES - 1
        cs_ref[k:k + 1, :] = xb_scr[r:r + 1, :]

    u_scr[...] = _reorder_rows(_perm_matrix(), _rms(hn_ref[...], g_ref[...]).astype(BF16))
    dt_scr[...] = _softplus(_dot(u_scr[...], wdt_ref[...]) + dtb_ref[...])
    slab_w = 2 * LANES
    slabs = list(range(0, D_INNER + SSM_CONV_DIM, slab_w))

    def emit_in_proj_slab():
        if not slabs:
            return
        c0 = slabs.pop(0)
        res = _dot(u_scr[...], wzx_ref[:, c0:c0 + slab_w])
        if c0 < D_INNER:
            z_scr[:, c0:c0 + slab_w] = res
        else:
            xc = slice(c0 - D_INNER, c0 - D_INNER + slab_w)
            edge = jnp.where(next_starts_seq, 0.0, xb_scr[tl:tl + SSM_EDGE, xc])
            xb_scr[SSM_EDGE:SSM_EDGE + tl, xc] = res
            xb_scr[0:SSM_EDGE, xc] = edge

    cw = 4 * LANES
    sub = lax.broadcasted_iota(jnp.int32, (SUBLANES, cw), 0)
    for cb0 in range(0, SSM_CONV_DIM, cw):
        cl = slice(cb0, cb0 + cw)
        for c in range(tl // q):
            base = c * q
            cur = xb_scr[base + SSM_EDGE:base + SSM_EDGE + q, cl]
            edges = []
            for m in range(nk, 0, -1):
                jm = PERM_VROWS - m
                cur_v = cur[jm * SUBLANES:(jm + 1) * SUBLANES, :]
                prv_v = xb_scr[base + (nk - m) * SUBLANES:base + (nk - m + 1) * SUBLANES, cl]
                edges.append(_edge_vrow(cur_v, prv_v, m, sub))
            hist = jnp.concatenate(edges + [cur], axis=0)
            conv = bcv_ref[:, cl] + cur * wcv_ref[nk:nk + 1, cl]
            for s in range(1, SSM_CONV_WIDTH):
                lo = SSM_EDGE - s * SUBLANES
                conv = conv + hist[lo:lo + q, :] * wcv_ref[nk - s:nk - s + 1, cl]
            act = _silu(conv)
            rows = slice(base, base + q)
            if cb0 < D_INNER:
                x_scr[rows, cl] = act
            elif cb0 < D_INNER + N_GROUPS * D_STATE:
                b_scr[rows, cb0 - D_INNER:cb0 - D_INNER + cw] = act.astype(BF16)
            else:
                c0 = cb0 - D_INNER - N_GROUPS * D_STATE
                c_scr[rows, c0:c0 + cw] = act.astype(BF16)
        if slabs[0] < D_INNER:
            emit_in_proj_slab()

    a_neg = -jnp.exp(alog_ref[...])
    causal = _time_of_row((q, q), 0) >= _time_of_row((q, q), 1)
    tril = causal.astype(BF16)
    low_half = lax.broadcasted_iota(jnp.int32, (q, LANES), 1) < HEAD_DIM
    blk = _lane_block_ids((q, GROUP_W), HEAD_DIM)

    def pair_expand(cols):
        return jnp.concatenate([jnp.where(low_half, cols[0], cols[1]),
                                jnp.where(low_half, cols[2], cols[3])], axis=-1)

    for c in range(tl // q):
        rows = slice(c * q, (c + 1) * q)
        dtc = dtc_scr[rows, :]
        a = dtc * a_neg
        a_hi, a_mid, a_lo = _split3(a)
        a_cum = _dot(tril, a_hi) + _dot(tril, a_mid) + _dot(tril, a_lo)
        a_cum_t = a_cum.T
        for g in range(N_GROUPS):
            gs = slice(g * GROUP_W, (g + 1) * GROUP_W)
            ns = slice(g * D_STATE, (g + 1) * D_STATE)
            heads = [g * HEADS_PER_GROUP + r for r in range(HEADS_PER_GROUP)]
            bg = b_scr[rows, ns]
            cg = c_scr[rows, ns]
            xg = x_scr[rows, gs]
            sg = s_scr[gs, :]
            if c == 0:
                sg = jnp.where(first_in_seq, 0.0, sg)
            cb = _dot_nt(cg, bg)
            y_off = _dot_nt(cg, sg.astype(BF16))
            bc_a = [jnp.broadcast_to(a_cum[:, hh:hh + 1], (q, LANES)) for hh in heads]
            bc_dt = [jnp.broadcast_to(dtc[:, hh:hh + 1], (q, LANES)) for hh in heads]
            a_e = pair_expand(bc_a)
            xdt = xg * pair_expand(bc_dt)
            ms = []
            for r, hh in enumerate(heads):
                seg = bc_a[r] - a_cum_t[hh:hh + 1, :]
                lm = jnp.exp(jnp.where(causal, seg, -jnp.inf))
                ms.append((cb * lm).astype(BF16))
            xblk = jnp.concatenate(
                [jnp.where(blk == r, xdt, 0.0).astype(BF16) for r in range(HEADS_PER_GROUP)], axis=0)
            y = (_dot(jnp.concatenate(ms, axis=-1), xblk) + y_off * jnp.exp(a_e)
                 + xg * dsk_ref[:, gs])
            y_scr[rows, gs] = y
            xs = (xdt * jnp.exp(a_e[q - 1:q, :] - a_e)).astype(BF16)
            add = _dot_tn(xs, bg)
            for r, hh in enumerate(heads):
                cd = jnp.exp(bc_a[r][q - 1:q, :])
                hs = slice(hh * HEAD_DIM, (hh + 1) * HEAD_DIM)
                ps = slice(r * HEAD_DIM, (r + 1) * HEAD_DIM)
                s_scr[hs, :] = cd * sg[ps, :] + add[ps, :]
            emit_in_proj_slab()

    while slabs:
        emit_in_proj_slab()
    st_ref[...] = s_scr[...]
    yn = _reorder_rows(_unperm_matrix(), _gated_norm(y_scr[...], gz_scr[...], ng_ref[...]))
    o_ref[...] = hc_ref[...] + _dot(yn, wout_ref[...])


def _mamba_weight_specs(mix_layer, layer):
    return [_layer((1, D_MODEL), mix_layer),
            pl.BlockSpec((None, D_MODEL, D_INNER + SSM_CONV_DIM), lambda *_: (layer, 0, 0)),
            _layer((D_MODEL, LANES), layer), _layer((1, LANES), layer), _layer((1, LANES), layer),
            _layer((SSM_CONV_WIDTH, SSM_CONV_DIM), layer), _layer((1, SSM_CONV_DIM), layer)]


def _mamba_prompt(h, g, wzx, wdt, dtb, alog, wcv, bcv, dsk, ng, wout, mix_layer, layer, tl):
    bsz, seq, _ = h.shape
    nt = seq // tl
    n_tiles = bsz * nt
    nconv = SSM_CONV_WIDTH - 1

    def cur(s):
        return jnp.maximum(s - 1, 0)

    def nxt(s):
        return jnp.minimum(s, n_tiles - 1)

    tile_cur = pl.BlockSpec((None, tl, D_MODEL), lambda s: (cur(s) // nt, cur(s) % nt, 0))
    tile_nxt = pl.BlockSpec((None, tl, D_MODEL), lambda s: (nxt(s) // nt, nxt(s) % nt, 0))
    return pl.pallas_call(
        functools.partial(_mamba_prompt_kernel, tl=tl, nt=nt),
        grid=(n_tiles + 1,),
        in_specs=[tile_nxt, tile_cur] + _mamba_weight_specs(mix_layer, layer) + [
            _layer((1, D_INNER), layer), _layer((1, D_INNER), layer),
            _layer((D_INNER, D_MODEL), layer)],
        out_specs=[tile_cur,
                   pl.BlockSpec((None, D_INNER, D_STATE), lambda s: (cur(s) // nt, 0, 0)),
                   pl.BlockSpec((None, nconv, SSM_CONV_DIM), lambda s: (cur(s) // nt, 0, 0))],
        out_shape=[jax.ShapeDtypeStruct(h.shape, F32),
                   jax.ShapeDtypeStruct((bsz, D_INNER, D_STATE), F32),
                   jax.ShapeDtypeStruct((bsz, nconv, SSM_CONV_DIM), F32)],
        scratch_shapes=[pltpu.VMEM((tl, D_INNER), F32),
                        pltpu.VMEM((tl + SSM_EDGE, SSM_CONV_DIM), F32),
                        pltpu.VMEM((tl, LANES), F32),
                        pltpu.VMEM((tl, D_MODEL), BF16),
                        pltpu.VMEM((tl, D_INNER), F32),
                        pltpu.VMEM((tl, LANES), F32),
                        pltpu.VMEM((tl, D_INNER), F32),
                        pltpu.VMEM((tl, N_GROUPS * D_STATE), BF16),
                        pltpu.VMEM((tl, N_GROUPS * D_STATE), BF16),
                        pltpu.VMEM((tl, D_INNER), F32),
                        pltpu.VMEM((D_INNER, D_STATE), F32)],
        compiler_params=_cparams(1),
        name="mamba_prompt",
    )(h, h, g, wzx, wdt, dtb, alog, wcv, bcv, dsk, ng, wout)


def _mamba_step_in_kernel(h_ref, g_ref, wzx_ref, wdt_ref, dtb_ref, alog_ref, wcv_ref, bcv_ref,
                          cs_ref, *rest, layer, fill):
    z_ref, act_ref, dt_ref, da_ref, ncs_ref = rest[-5:]
    nk = SSM_CONV_WIDTH - 1

    def body():
        u = _rms(h_ref[...], g_ref[...]).astype(BF16)
        z_ref[...] = _dot(u, wzx_ref[:, :D_INNER])
        xb = _dot(u, wzx_ref[:, D_INNER:])
        dt = _softplus(_dot(u, wdt_ref[...]) + dtb_ref[...])
        dt_ref[...] = dt
        da_ref[...] = jnp.exp(dt * (-jnp.exp(alog_ref[...])))
        conv = bcv_ref[...] + xb * wcv_ref[nk:nk + 1, :]
        for k in range(nk):
            conv = conv + cs_ref[k] * wcv_ref[k:k + 1, :]
        act_ref[...] = _silu(conv)
        ncs_ref[0:nk - 1] = cs_ref[1:nk]
        ncs_ref[nk - 1] = xb

    _own_slot_or_zero(fill, layer, ncs_ref, body)


def _mamba_step_in(h, g, wzx, wdt, dtb, alog, wcv, bcv, states, mix_layer, layer, prev_out):
    n_layers, nk, bsz, _ = states.shape
    fill = prev_out is None
    grid, _, li = _stack_plan(layer, n_layers, 1, fill)
    sblk = (None, nk, bsz, SSM_CONV_DIM)
    shapes = [(bsz, D_INNER), (bsz, SSM_CONV_DIM), (bsz, LANES), (bsz, LANES)]
    in_specs = [_full((bsz, D_MODEL))] + _mamba_weight_specs(mix_layer, layer)
    in_specs.append(pl.BlockSpec(sblk, lambda ls, i: (layer, 0, 0, 0)))
    args = [h, g, wzx, wdt, dtb, alog, wcv, bcv, states]
    aliases = {}
    if not fill:
        in_specs.append(pl.BlockSpec(memory_space=pl.ANY))
        args.append(prev_out)
        aliases = {len(args) - 1: len(shapes)}
    return pl.pallas_call(
        functools.partial(_mamba_step_in_kernel, layer=layer, fill=fill),
        grid=grid,
        in_specs=in_specs,
        out_specs=[_full(s) for s in shapes] + [pl.BlockSpec(sblk, lambda ls, i: (li(ls), 0, 0, 0))],
        out_shape=[jax.ShapeDtypeStruct(s, F32) for s in shapes]
        + [jax.ShapeDtypeStruct(states.shape, F32)],
        input_output_aliases=aliases,
        compiler_params=_cparams(2),
        name="mamba_step_in",
    )(*args)


def _ssm_step_kernel(da_ref, s_ref, act_ref, dt_ref, dsk_ref, *rest, tb, layer, fill):
    ns_ref, y_ref = rest[-2:]
    _own_slot_or_zero(
        fill, layer, ns_ref,
        lambda: _ssm_step_body(da_ref, s_ref, act_ref, dt_ref, dsk_ref, ns_ref, y_ref, tb))


def _ssm_step_body(da_ref, s_ref, act_ref, dt_ref, dsk_ref, ns_ref, y_ref, tb):
    i = pl.program_id(1)
    rid = lax.broadcasted_iota(jnp.int32, (tb, 1), 0)
    zpad = jnp.zeros((tb, GROUP_W), F32)
    dt = dt_ref[...]
    for g in range(N_GROUPS):
        gs = slice(g * GROUP_W, (g + 1) * GROUP_W)
        xg = act_ref[:, gs]
        bg = act_ref[:, D_INNER + g * D_STATE:D_INNER + (g + 1) * D_STATE]
        cg = act_ref[:, D_INNER + (N_GROUPS + g) * D_STATE:D_INNER + (N_GROUPS + g + 1) * D_STATE]
        xdt = xg * _expand_heads(dt, g, tb)
        y8 = jnp.zeros((tb, GROUP_W), F32)
        for r in range(tb):
            sel = rid == r
            x1 = jnp.concatenate([jnp.where(sel, xdt, 0.0), zpad], axis=0).astype(BF16)
            b1 = jnp.concatenate([jnp.where(sel, bg, 0.0), zpad[:, :D_STATE]], axis=0).astype(BF16)
            c1 = jnp.concatenate([jnp.where(sel, cg, 0.0), zpad[:, :D_STATE]], axis=0).astype(BF16)
            add = _dot_tn(x1, b1)
            news = []
            for k in range(HEADS_PER_GROUP):
                hh = g * HEADS_PER_GROUP + k
                da = da_ref[i * tb + r, hh]
                sn = da * s_ref[r, hh] + add[k * HEAD_DIM:(k + 1) * HEAD_DIM, :]
                ns_ref[r, hh] = sn
                news.append(sn)
            sg = jnp.concatenate(news, axis=0).astype(BF16)
            y8 = y8 + _dot_nt(c1, sg)[:tb, :]
        y_ref[:, gs] = y8 + xg * dsk_ref[:, gs]


def _ssm_step(da, states, layer, act, dt, dsk, tb, prev_out):
    n_layers, bsz = states.shape[:2]
    fill = prev_out is None
    grid, bi, li = _stack_plan(layer, n_layers, bsz // tb, fill)
    sblk = (None, tb, N_HEADS, HEAD_DIM, D_STATE)
    in_specs = [pl.BlockSpec(memory_space=pltpu.SMEM),
                pl.BlockSpec(sblk, lambda ls, i: (layer, bi(ls, i), 0, 0, 0)),
                pl.BlockSpec((tb, SSM_CONV_DIM), lambda ls, i: (bi(ls, i), 0)),
                pl.BlockSpec((tb, LANES), lambda ls, i: (bi(ls, i), 0)),
                _layer((1, D_INNER), layer)]
    args = [da, states, act, dt, dsk]
    aliases = {}
    if not fill:
        in_specs.append(pl.BlockSpec(memory_space=pl.ANY))
        args.append(prev_out)
        aliases = {len(args) - 1: 0}
    return pl.pallas_call(
        functools.partial(_ssm_step_kernel, tb=tb, layer=layer, fill=fill),
        grid=grid,
        in_specs=in_specs,
        out_specs=[pl.BlockSpec(sblk, lambda ls, i: (li(ls), i, 0, 0, 0)),
                   pl.BlockSpec((tb, D_INNER), lambda ls, i: (bi(ls, i), 0))],
        out_shape=[jax.ShapeDtypeStruct(states.shape, F32),
                   jax.ShapeDtypeStruct((bsz, D_INNER), F32)],
        input_output_aliases=aliases,
        compiler_params=_cparams(2),
        name="ssm_step",
    )(*args)


def _gnorm_ffn_kernel(y_ref, z_ref, h_ref, ng_ref, wout_ref, p_ref, gf_ref, wg_ref, wu_ref, wd_ref,
                      gp_ref, wpg_ref, wpp_ref, gfin_ref, o_ref, *, final):
    yn = _gated_norm(y_ref[...], _silu(z_ref[...]), ng_ref[...])
    h = h_ref[...] + _dot(yn, wout_ref[...])
    _ffn_ple_body(h, p_ref, gf_ref, wg_ref, wu_ref, wd_ref, gp_ref, wpg_ref, wpp_ref,
                  gfin_ref, o_ref, final)


def _gnorm_ffn(y, z, h, ng, wout, p, gf, wg, wu, wd, gp, wpg, wpp, gfin, layer, ffn_layer, final):
    bsz = y.shape[0]
    vec = _layer((1, D_MODEL), ffn_layer)
    return pl.pallas_call(
        functools.partial(_gnorm_ffn_kernel, final=final),
        grid=(1,),
        in_specs=[_full((bsz, D_INNER)), _full((bsz, D_INNER)), _full((bsz, D_MODEL)),
                  _layer((1, D_INNER), layer), _layer((D_INNER, D_MODEL), layer),
                  _layer((bsz, PLE_DIM), ffn_layer),
                  vec, _layer((D_MODEL, D_FF), ffn_layer), _layer((D_MODEL, D_FF), ffn_layer),
                  _layer((D_FF, D_MODEL), ffn_layer), vec, _layer((D_MODEL, D_MODEL), ffn_layer),
                  _layer((PLE_DIM, D_MODEL), ffn_layer), _full((1, D_MODEL))],
        out_specs=_full((bsz, D_MODEL)),
        out_shape=jax.ShapeDtypeStruct((bsz, D_MODEL), F32),
        compiler_params=_cparams(1),
        name="gnorm_ffn",
    )(y, z, h, ng, wout, p, gf, wg, wu, wd, gp, wpg, wpp, gfin)


def _rows(x):
    return x.reshape(x.shape[0], 1, x.shape[1])


def _pad_lanes(x):
    return jnp.pad(x, [(0, 0)] * (x.ndim - 1) + [(0, LANES - x.shape[-1])])


def kernel(x_prompt, x_sample, p_prompt, p_sample, state_conv_mixer, state_ssm, state_ssm_conv, g_mix, g_ffn, g_ple, g_final, cv_w_pw1, cv_b_pw1, cv_w_dw, cv_b_dw, cv_ln_g, cv_ln_b, cv_w_pw2, cv_b_pw2, ssm_w_in, ssm_w_conv, ssm_b_conv, ssm_dt_bias, ssm_a_log, ssm_d, ssm_norm_g, ssm_w_out, ffn_w_gate, ffn_w_up, ffn_w_down, ple_w_proj, ple_w_gate):
    depth = g_mix.shape[0]
    bp, seq, _ = x_prompt.shape
    bs = x_sample.shape[0]
    mp = bp * seq
    tm_prompt = 512
    tm_ffn = 1024
    tl_conv = 512
    tl_ssm = 256
    zx_w = D_INNER + SSM_CONV_DIM

    gm, gf, gp = _rows(g_mix), _rows(g_ffn), _rows(g_ple)
    gfin = g_final.reshape(1, D_MODEL)
    w1, b1 = cv_w_pw1.astype(BF16), _rows(cv_b_pw1)
    cargs = (cv_w_dw, _rows(cv_b_dw), _rows(cv_ln_g), _rows(cv_ln_b),
             cv_w_pw2.astype(BF16), _rows(cv_b_pw2))
    w_in = ssm_w_in.astype(BF16)
    margs = (w_in, _pad_lanes(ssm_w_in[:, :, zx_w:]).astype(BF16), _rows(_pad_lanes(ssm_dt_bias)),
             _rows(_pad_lanes(ssm_a_log)), ssm_w_conv, _rows(ssm_b_conv))
    dsk = _rows(jnp.repeat(ssm_d, HEAD_DIM, axis=1))
    ng = _rows(ssm_norm_g)
    wout = ssm_w_out.astype(BF16)
    fargs = (gf, ffn_w_gate.astype(BF16), ffn_w_up.astype(BF16), ffn_w_down.astype(BF16), gp,
             ple_w_gate.astype(BF16), ple_w_proj.astype(BF16), gfin)
    pp = p_prompt.reshape(depth, mp, PLE_DIM)
    ps = p_sample.reshape(depth, bs, PLE_DIM)
    conv_st = jnp.swapaxes(state_conv_mixer, 1, 2)
    ssmc_st = jnp.swapaxes(state_ssm_conv, 1, 2)

    hp = x_prompt
    hs = x_sample.reshape(bs, D_MODEL)
    conv_p, ssm_p, ssmc_p = [], [], []
    conv_s = ssm_s = ssmc_s = None

    for i in range(depth):
        j = i // 2
        if i % 2 == 0:
            vp = _pw1_glu(hp.reshape(mp, D_MODEL), gm, w1, b1, i, j, tm_prompt, True)
            vp = vp.reshape(bp, seq, D_MODEL)
            tail = vp[:, seq - PERM_ROWS:, :].reshape(bp, PERM_VROWS, SUBLANES, D_MODEL)
            tail = jnp.swapaxes(tail, 1, 2).reshape(bp, PERM_ROWS, D_MODEL)
            conv_p.append(tail[:, PERM_ROWS - (CONV_WIDTH - 1):, :])
            hp = _conv_prompt(vp, hp, *cargs, j, tl_conv)
            vs = _pw1_glu(hs, gm, w1, b1, i, j, bs, False)
            hs, conv_s = _conv_step(vs, hs, conv_st, *cargs, j, 32, conv_s)
        else:
            hp, fin, ncs = _mamba_prompt(hp, gm, *margs, dsk, ng, wout, i, j, tl_ssm)
            ssm_p.append(fin.reshape(bp, N_HEADS, HEAD_DIM, D_STATE))
            ssmc_p.append(ncs)
            z, act, dt, da, ssmc_s = _mamba_step_in(hs, gm, *margs, ssmc_st, i, j, ssmc_s)
            ssm_s, y = _ssm_step(da[:, :N_HEADS], state_ssm, j, act, dt, dsk, 8, ssm_s)
        final = i == depth - 1
        hp = _ffn_ple(hp.reshape(mp, D_MODEL), pp, *fargs, i, tm_ffn, final)
        hp = hp.reshape(bp, seq, D_MODEL)
        if i % 2 == 0:
            hs = _ffn_ple(hs, ps, *fargs, i, bs, final)
        else:
            hs = _gnorm_ffn(y, z, hs, ng, wout, ps, *fargs, j, i, final)

    return (hp, hs.reshape(bs, 1, D_MODEL),
            jnp.stack(conv_p), jnp.stack(ssm_p), jnp.stack(ssmc_p),
            jnp.swapaxes(conv_s, 1, 2), ssm_s, jnp.swapaxes(ssmc_s, 1, 2))
```
